```python
import math
import jax, jax.numpy as jnp
from jax import lax
import numpy as np

D_MODEL = 2048
BATCH = 8
SEQ = 2048
DEPTH = 1

Q_BLOCK = 128
ROPE_THETA = 10000.0
NORM_EPS = 1e-6

MLA_HEADS = 8
MLA_NOPE_DIM = 128
MLA_ROPE_DIM = 64
MLA_V_DIM = 128
MLA_Q_LORA = 512
MLA_KV_LORA = 256
MLA_OUT = MLA_HEADS * MLA_V_DIM

DIFF_HEADS = 4
DIFF_HEAD_DIM = 128
DIFF_V_DIM = 2 * DIFF_HEAD_DIM
DIFF_W = DIFF_HEADS * 2 * DIFF_HEAD_DIM
DIFF_OUT = DIFF_HEADS * DIFF_V_DIM

MIX_WIDTH = MLA_OUT + DIFF_OUT

IN_SPLITS = (MLA_Q_LORA, MLA_KV_LORA, MLA_ROPE_DIM, DIFF_W, DIFF_W, DIFF_W)
IN_WIDTH = sum(IN_SPLITS)

MEM_TOKENS = 256
X_HEADS = 4
X_HEAD_DIM = 128
X_WIDTH = X_HEADS * X_HEAD_DIM

FFN_HIDDEN = int(math.ceil(8 * D_MODEL / 3 / 256) * 256)

kernel_name = "hybrid_mla_diffattn_parallel_heads_encoder"


def rms_norm(x, g, eps=NORM_EPS):
    xf = x.astype(jnp.float32)
    y = xf * lax.rsqrt(jnp.mean(xf * xf, axis=-1, keepdims=True) + eps)
    return (y * g.astype(jnp.float32)).astype(x.dtype)


def rope_tables(positions, dim):
    inv_freq = ROPE_THETA ** (-jnp.arange(0, dim, 2, dtype=jnp.float32) / dim)
    ang = positions.astype(jnp.float32)[..., None] * inv_freq
    return jnp.cos(ang), jnp.sin(ang)


def apply_rope(t, cos, sin):
    shp = cos.shape[:2] + (1,) * (t.ndim - 3) + cos.shape[-1:]
    c, s = cos.reshape(shp), sin.reshape(shp)
    half = t.shape[-1] // 2
    t1 = t[..., :half].astype(jnp.float32)
    t2 = t[..., half:].astype(jnp.float32)
    return jnp.concatenate([t1 * c - t2 * s, t1 * s + t2 * c], axis=-1).astype(t.dtype)


def _to_blocks(t):
    b, s = t.shape[:2]
    return jnp.moveaxis(t.reshape((b, s // Q_BLOCK, Q_BLOCK) + t.shape[2:]), 1, 0)


def _from_blocks(t):
    t = jnp.moveaxis(t, 0, 1)
    return t.reshape((t.shape[0], t.shape[1] * t.shape[2]) + t.shape[3:])


def mla_attention(q_nope, q_rope, k_nope, k_rope, v):
    scale = (MLA_NOPE_DIM + MLA_ROPE_DIM) ** -0.5

    def block(qs):
        qn_b, qr_b = qs
        s = (jnp.einsum('bqhd,bkhd->bhqk', qn_b, k_nope)
             + jnp.einsum('bqhr,bkr->bhqk', qr_b, k_rope))
        p = jax.nn.softmax(s.astype(jnp.float32) * scale, axis=-1).astype(v.dtype)
        return jnp.einsum('bhqk,bkhd->bqhd', p, v)

    return _from_blocks(lax.map(block, (_to_blocks(q_nope), _to_blocks(q_rope))))


def diff_attention(q1, q2, k1, k2, v, lam):
    scale = DIFF_HEAD_DIM ** -0.5

    def block(qs):
        q1_b, q2_b = qs
        p1 = jax.nn.softmax(jnp.einsum('bqhd,bkhd->bhqk', q1_b, k1).astype(jnp.float32) * scale, axis=-1)
        p2 = jax.nn.softmax(jnp.einsum('bqhd,bkhd->bhqk', q2_b, k2).astype(jnp.float32) * scale, axis=-1)
        p = (p1 - lam * p2).astype(v.dtype)
        return jnp.einsum('bhqk,bkhe->bqhe', p, v)

    return _from_blocks(lax.map(block, (_to_blocks(q1), _to_blocks(q2))))


def setup_inputs(seed: int = 0) -> dict:
    key = jax.random.key(seed)
    ks = iter(jax.random.split(key, 40))

    def dense(shape):
        return jax.random.normal(next(ks), shape, jnp.float32) * (shape[-2] ** -0.5)

    def gain(n):
        return 1.0 + 0.02 * jax.random.normal(next(ks), (DEPTH, n), jnp.float32)

    x = jax.random.normal(next(ks), (BATCH, SEQ, D_MODEL), jnp.float32)
    mem = jax.random.normal(next(ks), (BATCH, MEM_TOKENS, D_MODEL), jnp.float32)
    offsets = jax.random.randint(next(ks), (BATCH, 1), 0, 1024, dtype=jnp.int32)
    positions = jnp.arange(SEQ, dtype=jnp.int32)[None, :] + offsets

    return {
        "x": x,
        "mem": mem,
        "positions": positions,
        "g_mix": gain(D_MODEL),
        "w_in": dense((DEPTH, D_MODEL, IN_WIDTH)),
        "g_q_lat": gain(MLA_Q_LORA),
        "w_uq": dense((DEPTH, MLA_Q_LORA, MLA_HEADS * (MLA_NOPE_DIM + MLA_ROPE_DIM))),
        "g_kv_lat": gain(MLA_KV_LORA),
        "w_ukv": dense((DEPTH, MLA_KV_LORA, MLA_HEADS * (MLA_NOPE_DIM + MLA_V_DIM))),
        "lambda_q1": 0.1 * jax.random.normal(next(ks), (DEPTH, DIFF_HEAD_DIM), jnp.float32),
        "lambda_k1": 0.1 * jax.random.normal(next(ks), (DEPTH, DIFF_HEAD_DIM), jnp.float32),
        "lambda_q2": 0.1 * jax.random.normal(next(ks), (DEPTH, DIFF_HEAD_DIM), jnp.float32),
        "lambda_k2": 0.1 * jax.random.normal(next(ks), (DEPTH, DIFF_HEAD_DIM), jnp.float32),
        "g_diff_sub": gain(DIFF_V_DIM),
        "w_out": dense((DEPTH, MIX_WIDTH, D_MODEL)),
        "g_xattn": gain(D_MODEL),
        "g_mem": gain(D_MODEL),
        "w_xq": dense((DEPTH, D_MODEL, X_WIDTH)),
        "w_xk": dense((DEPTH, D_MODEL, X_WIDTH)),
        "w_xv": dense((DEPTH, D_MODEL, X_WIDTH)),
        "w_xo": dense((DEPTH, X_WIDTH, D_MODEL)),
        "g_ffn": gain(D_MODEL),
        "w_gate": dense((DEPTH, D_MODEL, FFN_HIDDEN)),
        "w_up": dense((DEPTH, D_MODEL, FFN_HIDDEN)),
        "w_down": dense((DEPTH, FFN_HIDDEN, D_MODEL)),
        "g_final": 1.0 + 0.02 * jax.random.normal(next(ks), (D_MODEL,), jnp.float32),
    }


def reference(x, mem, positions, g_mix, w_in, g_q_lat, w_uq, g_kv_lat, w_ukv,
              lambda_q1, lambda_k1, lambda_q2, lambda_k2, g_diff_sub, w_out,
              g_xattn, g_mem, w_xq, w_xk, w_xv, w_xo,
              g_ffn, w_gate, w_up, w_down, g_final):
    b, s, _ = x.shape
    m_len = mem.shape[1]
    cos_r, sin_r = rope_tables(positions, MLA_ROPE_DIM)
    cos_d, sin_d = rope_tables(positions, DIFF_HEAD_DIM)
    split_pts = [int(v) for v in np.cumsum(IN_SPLITS)[:-1]]

    for layer in range(DEPTH):
        h = rms_norm(x, g_mix[layer])
        proj = h @ w_in[layer]
        c_q, c_kv, k_rope, dq, dk, dv = jnp.split(proj, split_pts, axis=-1)

        q = (rms_norm(c_q, g_q_lat[layer]) @ w_uq[layer]).reshape(b, s, MLA_HEADS, MLA_NOPE_DIM + MLA_ROPE_DIM)
        q_nope = q[..., :MLA_NOPE_DIM]
        q_rope = apply_rope(q[..., MLA_NOPE_DIM:], cos_r, sin_r)
        kv = (rms_norm(c_kv, g_kv_lat[layer]) @ w_ukv[layer]).reshape(b, s, MLA_HEADS, MLA_NOPE_DIM + MLA_V_DIM)
        k_nope, v_mla = kv[..., :MLA_NOPE_DIM], kv[..., MLA_NOPE_DIM:]
        k_rope = apply_rope(k_rope, cos_r, sin_r)
        out_mla = mla_attention(q_nope, q_rope, k_nope, k_rope, v_mla).reshape(b, s, MLA_OUT)

        dq = apply_rope(dq.reshape(b, s, DIFF_HEADS, 2, DIFF_HEAD_DIM), cos_d, sin_d)
        dk = apply_rope(dk.reshape(b, s, DIFF_HEADS, 2, DIFF_HEAD_DIM), cos_d, sin_d)
        dv = dv.reshape(b, s, DIFF_HEADS, DIFF_V_DIM)
        lambda_init = 0.8 - 0.6 * math.exp(-0.3 * layer)
        lam = (jnp.exp(jnp.sum(lambda_q1[layer].astype(jnp.float32) * lambda_k1[layer].astype(jnp.float32)))
               - jnp.exp(jnp.sum(lambda_q2[layer].astype(jnp.float32) * lambda_k2[layer].astype(jnp.float32)))
               + lambda_init)
        o_diff = diff_attention(dq[..., 0, :], dq[..., 1, :], dk[..., 0, :], dk[..., 1, :], dv, lam)
        o_diff = rms_norm(o_diff, g_diff_sub[layer], eps=1e-5) * (1.0 - lambda_init)
        out_diff = o_diff.reshape(b, s, DIFF_OUT)

        x = x + jnp.concatenate([out_mla, out_diff], axis=-1) @ w_out[layer]

        hx = rms_norm(x, g_xattn[layer])
        hm = rms_norm(mem, g_mem[layer])
        xq = (hx @ w_xq[layer]).reshape(b, s, X_HEADS, X_HEAD_DIM)
        xk = (hm @ w_xk[layer]).reshape(b, m_len, X_HEADS, X_HEAD_DIM)
        xv = (hm @ w_xv[layer]).reshape(b, m_len, X_HEADS, X_HEAD_DIM)
        sc = jnp.einsum('bqhd,bkhd->bhqk', xq, xk).astype(jnp.float32) * (X_HEAD_DIM ** -0.5)
        p = jax.nn.softmax(sc, axis=-1).astype(xv.dtype)
        xo = jnp.einsum('bhqk,bkhd->bqhd', p, xv).reshape(b, s, X_WIDTH)
        x = x + xo @ w_xo[layer]

        hf = rms_norm(x, g_ffn[layer])
        x = x + (jax.nn.silu(hf @ w_gate[layer]) * (hf @ w_up[layer])) @ w_down[layer]

    return rms_norm(x, g_final)
```

```python
import functools
import math

import numpy as np
import jax
import jax.numpy as jnp
from jax import lax
from jax.experimental import pallas as pl
from jax.experimental.pallas import tpu as pltpu

F32 = jnp.float32
BF16 = jnp.bfloat16

ROPE_THETA = 10000.0
NORM_EPS = 1e-6
DIFF_NORM_EPS = 1e-5
Q_LORA, KV_LORA, ROPE_DIM = 512, 256, 64
MLA_HEADS, NOPE_DIM, V_DIM = 8, 128, 128
DIFF_HEADS, DIFF_DIM = 4, 128
X_HEADS, X_DIM = 4, 128
LANES = 128
LOG2E = math.log2(math.e)
VMEM_LIMIT = 58 * 1024 * 1024

LAT_PAD = 1024
MLA_QK = 2 * LANES


def _rms(x, g, eps):
    return x * lax.rsqrt(jnp.mean(x * x, axis=-1, keepdims=True) + eps) * g


def _nt_dot(a, b):
    return lax.dot_general(a, b, (((1,), (1,)), ((), ())), preferred_element_type=F32)


def _softmax_parts(s):
    m = jnp.max(s, axis=-1, keepdims=True)
    e = jnp.exp2(s - m)
    return e, jnp.sum(e, axis=-1, keepdims=True)


def _rope_table_kernel(pos_ref, freq_ref, cosd_ref, sind_ref, cosm_ref, sinlo_ref, sinhi_ref):
    ang = pos_ref[...].astype(F32) * freq_ref[...]
    ct, st = jnp.cos(ang), jnp.sin(ang)
    ct_r, st_r = pltpu.roll(ct, 64, 1), pltpu.roll(st, 64, 1)
    lane = lax.broadcasted_iota(jnp.int32, ang.shape, 1)
    lo = lane < 64
    cosd_ref[...] = jnp.where(lo, ct, ct_r)
    sind_ref[...] = jnp.where(lo, -st, st_r)
    cosm_ref[...] = jnp.where(lo, ct_r, ct)
    sm = jnp.where(lo, st_r, st)
    first_half = (lane & 63) < 32
    sinlo_ref[...] = jnp.where(first_half, -sm, 0.0)
    sinhi_ref[...] = jnp.where(first_half, 0.0, sm)


def _rope_tables(positions, tile):
    n = positions.size
    inv_d = ROPE_THETA ** (-jnp.arange(0, DIFF_DIM, 2, dtype=F32) / DIFF_DIM)
    inv_m = ROPE_THETA ** (-jnp.arange(0, ROPE_DIM, 2, dtype=F32) / ROPE_DIM)
    freq = jnp.concatenate([inv_d, inv_m, inv_m]).reshape(1, LANES)
    tab = jax.ShapeDtypeStruct((n, LANES), F32)
    tspec = pl.BlockSpec((tile, LANES), lambda i: (i, 0))
    return pl.pallas_call(
        _rope_table_kernel,
        grid=(n // tile,),
        in_specs=[pl.BlockSpec((tile, 1), lambda i: (i, 0)), pl.BlockSpec((1, LANES), lambda i: (0, 0))],
        out_specs=[tspec] * 5,
        out_shape=[tab] * 5,
        name="rope_tables",
    )(positions.reshape(n, 1), freq)


def _in_proj_kernel(x_ref, gmix_ref, w_ref, gq_ref, wuq_ref, gkv_ref, wukv_ref,
                    cosd_ref, sind_ref, cosm_ref, sinlo_ref, sinhi_ref,
                    q_ref, k_ref, v_ref, dq_ref, dk_ref, dv_ref, h_ref, acc_ref,
                    *, mla_scale, diff_scale):
    j = pl.program_id(1)

    @pl.when(j == 0)
    def _():
        h_ref[...] = _rms(x_ref[...], gmix_ref[...], NORM_EPS).astype(BF16)

    acc_ref[...] = jnp.dot(h_ref[...], w_ref[...], preferred_element_type=F32)

    def rope64(t):
        return (t * cosm_ref[...] + pltpu.roll(t, 96, 1) * sinlo_ref[...]
                + pltpu.roll(t, 32, 1) * sinhi_ref[...])

    def rope128(t):
        return t * cosd_ref[...] + pltpu.roll(t, 64, 1) * sind_ref[...]

    @pl.when(j == 0)
    def _():
        c_q = _rms(acc_ref[:, :Q_LORA], gq_ref[...], NORM_EPS).astype(BF16)
        q = jnp.dot(c_q, wuq_ref[...], preferred_element_type=F32) * mla_scale
        nope_w = MLA_HEADS * NOPE_DIM
        q_rope = [rope64(q[:, nope_w + c * LANES: nope_w + (c + 1) * LANES]).astype(BF16)
                  for c in range(MLA_HEADS // 2)]
        c_kv = _rms(acc_ref[:, Q_LORA:Q_LORA + KV_LORA], gkv_ref[...], NORM_EPS).astype(BF16)
        kv = jnp.dot(c_kv, wukv_ref[...], preferred_element_type=F32)
        kr = rope64(acc_ref[:, Q_LORA + KV_LORA:Q_LORA + KV_LORA + LANES])
        kr_even, kr_odd = kr.astype(BF16), pltpu.roll(kr, 64, 1).astype(BF16)
        for h in range(MLA_HEADS):
            base = h * MLA_QK
            q_ref[:, base:base + NOPE_DIM] = q[:, h * NOPE_DIM:(h + 1) * NOPE_DIM].astype(BF16)
            q_ref[:, base + NOPE_DIM:base + MLA_QK] = q_rope[h // 2]
            k_ref[:, base:base + NOPE_DIM] = kv[:, base:base + NOPE_DIM].astype(BF16)
            k_ref[:, base + NOPE_DIM:base + MLA_QK] = kr_even if h % 2 == 0 else kr_odd
            v_ref[:, h * V_DIM:(h + 1) * V_DIM] = kv[:, base + NOPE_DIM:base + MLA_QK].astype(BF16)

    @pl.when(j == 1)
    def _():
        for c in range(acc_ref.shape[1] // LANES):
            sl = slice(c * LANES, (c + 1) * LANES)
            dq_ref[:, sl] = (rope128(acc_ref[:, sl]) * diff_scale).astype(BF16)

    @pl.when(j == 2)
    def _():
        for c in range(acc_ref.shape[1] // LANES):
            sl = slice(c * LANES, (c + 1) * LANES)
            dk_ref[:, sl] = rope128(acc_ref[:, sl]).astype(BF16)

    @pl.when(j == 3)
    def _():
        dv_ref[...] = acc_ref[...].astype(BF16)


def _in_proj(x2d, g_mix, w_in_b, g_q, w_uq_b, g_kv, w_ukv_b, tables, tm):
    n, d = x2d.shape
    nj = w_in_b.shape[1] // LAT_PAD
    diff_w = 2 * DIFF_HEADS * DIFF_DIM
    row = lambda i, j: (i, 0)
    const = lambda i, j: (0, 0)
    kern = functools.partial(
        _in_proj_kernel,
        mla_scale=(NOPE_DIM + ROPE_DIM) ** -0.5 * LOG2E,
        diff_scale=DIFF_DIM ** -0.5 * LOG2E)
    outs = [
        jax.ShapeDtypeStruct((n, MLA_HEADS * MLA_QK), BF16),
        jax.ShapeDtypeStruct((n, MLA_HEADS * MLA_QK), BF16),
        jax.ShapeDtypeStruct((n, MLA_HEADS * V_DIM), BF16),
        jax.ShapeDtypeStruct((n, diff_w), BF16),
        jax.ShapeDtypeStruct((n, diff_w), BF16),
        jax.ShapeDtypeStruct((n, diff_w), BF16),
    ]
    return pl.pallas_call(
        kern,
        grid=(n // tm, nj),
        in_specs=[
            pl.BlockSpec((tm, d), row),
            pl.BlockSpec((1, d), const),
            pl.BlockSpec((d, LAT_PAD), lambda i, j: (0, j)),
            pl.BlockSpec((1, Q_LORA), const),
            pl.BlockSpec(w_uq_b.shape, const),
            pl.BlockSpec((1, KV_LORA), const),
            pl.BlockSpec(w_ukv_b.shape, const),
        ] + [pl.BlockSpec((tm, LANES), row)] * 5,
        out_specs=[pl.BlockSpec((tm, o.shape[1]), row) for o in outs],
        out_shape=outs,
        scratch_shapes=[pltpu.VMEM((tm, d), BF16), pltpu.VMEM((tm, LAT_PAD), F32)],
        compiler_params=pltpu.CompilerParams(
            dimension_semantics=("parallel", "arbitrary"), vmem_limit_bytes=VMEM_LIMIT),
        name="in_proj",
    )(x2d, g_mix, w_in_b, g_q, w_uq_b, g_kv, w_ukv_b, *tables)


def _mla_attn_kernel(q_ref, k_ref, v_ref, o_ref, *, tq):
    def body(i, carry):
        rows = pl.ds(pl.multiple_of(i * tq, tq), tq)
        e, l = _softmax_parts(_nt_dot(q_ref[rows, :], k_ref[...]))
        o = jnp.dot(e.astype(BF16), v_ref[...], preferred_element_type=F32)
        o_ref[rows, :] = (o * (1.0 / l)).astype(o_ref.dtype)
        return carry

    lax.fori_loop(0, q_ref.shape[0] // tq, body, 0)


def _mla_attn(q, k, v, batch, seq, tq):
    blk = lambda w: pl.BlockSpec((seq, w), lambda b, h: (b, h))
    return pl.pallas_call(
        functools.partial(_mla_attn_kernel, tq=tq),
        grid=(batch, MLA_HEADS),
        in_specs=[blk(MLA_QK), blk(MLA_QK), blk(V_DIM)],
        out_specs=blk(V_DIM),
        out_shape=jax.ShapeDtypeStruct(v.shape, BF16),
        compiler_params=pltpu.CompilerParams(
            dimension_semantics=("parallel", "parallel"), vmem_limit_bytes=VMEM_LIMIT),
        name="mla_attn",
    )(q, k, v)


def _diff_attn_kernel(lq1_ref, lk1_ref, lq2_ref, lk2_ref, q_ref, k_ref, v_ref, g_ref, o_ref,
                      *, tq, lambda_init):
    lam = (jnp.exp(jnp.sum(lq1_ref[...] * lk1_ref[...], axis=-1, keepdims=True))
           - jnp.exp(jnp.sum(lq2_ref[...] * lk2_ref[...], axis=-1, keepdims=True))
           + lambda_init)

    def body(i, carry):
        rows = pl.ds(pl.multiple_of(i * tq, tq), tq)
        e1, l1 = _softmax_parts(_nt_dot(q_ref[rows, :DIFF_DIM], k_ref[:, :DIFF_DIM]))
        e2, l2 = _softmax_parts(_nt_dot(q_ref[rows, DIFF_DIM:], k_ref[:, DIFF_DIM:]))
        p = e1 * (1.0 / l1) - e2 * (lam / l2)
        o = jnp.dot(p.astype(BF16), v_ref[...], preferred_element_type=F32)
        o = _rms(o, g_ref[...], DIFF_NORM_EPS) * (1.0 - lambda_init)
        o_ref[rows, :] = o.astype(o_ref.dtype)
        return carry

    lax.fori_loop(0, q_ref.shape[0] // tq, body, 0)


def _diff_attn(lams, dq, dk, dv, g_sub, batch, seq, tq, lambda_init):
    w = 2 * DIFF_DIM
    blk = pl.BlockSpec((seq, w), lambda b, h: (b, h))
    vec = lambda width: pl.BlockSpec((1, width), lambda b, h: (0, 0))
    return pl.pallas_call(
        functools.partial(_diff_attn_kernel, tq=tq, lambda_init=lambda_init),
        grid=(batch, DIFF_HEADS),
        in_specs=[vec(DIFF_DIM)] * 4 + [blk, blk, blk, vec(w)],
        out_specs=blk,
        out_shape=jax.ShapeDtypeStruct(dv.shape, BF16),
        compiler_params=pltpu.CompilerParams(
            dimension_semantics=("parallel", "parallel"), vmem_limit_bytes=VMEM_LIMIT),
        name="diff_attn",
    )(*lams, dq, dk, dv, g_sub)


def _mem_kv_kernel(mem_ref, g_ref, wk_ref, wv_ref, k_ref, v_ref):
    hm = _rms(mem_ref[...], g_ref[...], NORM_EPS).astype(BF16)
    k_ref[...] = jnp.dot(hm, wk_ref[...], preferred_element_type=F32).astype(BF16)
    v_ref[...] = jnp.dot(hm, wv_ref[...], preferred_element_type=F32).astype(BF16)


def _mem_kv(mem2d, g_mem, w_xk_b, w_xv_b, m_len):
    n, d = mem2d.shape
    xw = w_xk_b.shape[1]
    const = lambda b: (0, 0)
    out = jax.ShapeDtypeStruct((n, xw), BF16)
    return pl.pallas_call(
        _mem_kv_kernel,
        grid=(n // m_len,),
        in_specs=[pl.BlockSpec((m_len, d), lambda b: (b, 0)), pl.BlockSpec((1, d), const),
                  pl.BlockSpec((d, xw), const), pl.BlockSpec((d, xw), const)],
        out_specs=[pl.BlockSpec((m_len, xw), lambda b: (b, 0))] * 2,
        out_shape=[out, out],
        compiler_params=pltpu.CompilerParams(
            dimension_semantics=("parallel",), vmem_limit_bytes=VMEM_LIMIT),
        name="mem_kv",
    )(mem2d, g_mem, w_xk_b, w_xv_b)


def _out_xattn_kernel(x_ref, a_ref, b_ref, wo_ref, gx_ref, wxq_ref, xk_ref, xv_ref, wxo_ref, o_ref,
                      *, x_scale):
    half = a_ref.shape[1]
    x1 = (x_ref[...]
          + jnp.dot(a_ref[...], wo_ref[:half, :], preferred_element_type=F32)
          + jnp.dot(b_ref[...], wo_ref[half:, :], preferred_element_type=F32))
    hx = _rms(x1, gx_ref[...], NORM_EPS).astype(BF16)
    xq = (jnp.dot(hx, wxq_ref[...], preferred_element_type=F32) * x_scale).astype(BF16)
    heads = []
    for h in range(X_HEADS):
        sl = slice(h * X_DIM, (h + 1) * X_DIM)
        e, l = _softmax_parts(_nt_dot(xq[:, sl], xk_ref[:, sl]))
        o = jnp.dot(e.astype(BF16), xv_ref[:, sl], preferred_element_type=F32)
        heads.append((o * (1.0 / l)).astype(BF16))
    xo = jnp.concatenate(heads, axis=-1)
    o_ref[...] = x1 + jnp.dot(xo, wxo_ref[...], preferred_element_type=F32)


def _out_xattn(x2d, a, b, w_out_b, g_x, w_xq_b, xk, xv, w_xo_b, seq, m_len, tm):
    n, d = x2d.shape
    xw = w_xq_b.shape[1]
    per_b = seq // tm
    row = lambda i: (i, 0)
    const = lambda i: (0, 0)
    mem = lambda i: (i // per_b, 0)
    return pl.pallas_call(
        functools.partial(_out_xattn_kernel, x_scale=X_DIM ** -0.5 * LOG2E),
        grid=(n // tm,),
        in_specs=[
            pl.BlockSpec((tm, d), row),
            pl.BlockSpec((tm, a.shape[1]), row),
            pl.BlockSpec((tm, b.shape[1]), row),
            pl.BlockSpec(w_out_b.shape, const),
            pl.BlockSpec((1, d), const),
            pl.BlockSpec(w_xq_b.shape, const),
            pl.BlockSpec((m_len, xw), mem),
            pl.BlockSpec((m_len, xw), mem),
            pl.BlockSpec(w_xo_b.shape, const),
        ],
        out_specs=pl.BlockSpec((tm, d), row),
        out_shape=jax.ShapeDtypeStruct((n, d), F32),
        compiler_params=pltpu.CompilerParams(
            dimension_semantics=("parallel",), vmem_limit_bytes=VMEM_LIMIT),
        name="out_xattn",
    )(x2d, a, b, w_out_b, g_x, w_xq_b, xk, xv, w_xo_b)


def _ffn_kernel(x_ref, g_ref, wg_ref, wu_ref, wd_ref, gf_ref, o_ref, h_ref, acc_ref, *, final_norm):
    j = pl.program_id(1)

    @pl.when(j == 0)
    def _():
        x = x_ref[...]
        h_ref[...] = _rms(x, g_ref[...], NORM_EPS).astype(BF16)
        acc_ref[...] = x

    h = h_ref[...]
    gate = jnp.dot(h, wg_ref[...], preferred_element_type=F32)
    up = jnp.dot(h, wu_ref[...], preferred_element_type=F32)
    act = (gate * jax.nn.sigmoid(gate) * up).astype(BF16)
    acc_ref[...] += jnp.dot(act, wd_ref[...], preferred_element_type=F32)

    @pl.when(j == pl.num_programs(1) - 1)
    def _():
        y = acc_ref[...]
        o_ref[...] = _rms(y, gf_ref[...], NORM_EPS) if final_norm else y


def _ffn(x2d, g_ffn, w_gate_b, w_up_b, w_down_b, g_final, tm, th, final_norm):
    n, d = x2d.shape
    hidden = w_gate_b.shape[1]
    row = lambda i, j: (i, 0)
    const = lambda i, j: (0, 0)
    return pl.pallas_call(
        functools.partial(_ffn_kernel, final_norm=final_norm),
        grid=(n // tm, hidden // th),
        in_specs=[
            pl.BlockSpec((tm, d), row),
            pl.BlockSpec((1, d), const),
            pl.BlockSpec((d, th), lambda i, j: (0, j)),
            pl.BlockSpec((d, th), lambda i, j: (0, j)),
            pl.BlockSpec((th, d), lambda i, j: (j, 0)),
            pl.BlockSpec((1, d), const),
        ],
        out_specs=pl.BlockSpec((tm, d), row),
        out_shape=jax.ShapeDtypeStruct((n, d), F32),
        scratch_shapes=[pltpu.VMEM((tm, d), BF16), pltpu.VMEM((tm, d), F32)],
        compiler_params=pltpu.CompilerParams(
            dimension_semantics=("parallel", "arbitrary"), vmem_limit_bytes=VMEM_LIMIT),
        name="ffn",
    )(x2d, g_ffn, w_gate_b, w_up_b, w_down_b, g_final)


def kernel(x, mem, positions, g_mix, w_in, g_q_lat, w_uq, g_kv_lat, w_ukv, lambda_q1, lambda_k1,
           lambda_q2, lambda_k2, g_diff_sub, w_out, g_xattn, g_mem, w_xq, w_xk, w_xv, w_xo,
           g_ffn, w_gate, w_up, w_down, g_final):
    batch, seq, d = x.shape
    m_len = mem.shape[1]
    depth = w_in.shape[0]
    n = batch * seq
    lat_w = Q_LORA + KV_LORA + ROPE_DIM

    tables = _rope_tables(positions, tile=2048)

    head_w = NOPE_DIM + ROPE_DIM
    uq_cols = np.concatenate(
        [h * head_w + np.arange(NOPE_DIM) for h in range(MLA_HEADS)]
        + [h * head_w + NOPE_DIM + np.arange(ROPE_DIM) for h in range(MLA_HEADS)])

    x2d = x.reshape(n, d)
    mem2d = mem.reshape(batch * m_len, d)
    row = lambda v: v.reshape(1, -1)

    for layer in range(depth):
        lambda_init = 0.8 - 0.6 * math.exp(-0.3 * layer)
        w_in_l = w_in[layer]
        w_in_b = jnp.concatenate(
            [w_in_l[:, :lat_w], jnp.zeros((d, LAT_PAD - lat_w), F32), w_in_l[:, lat_w:]],
            axis=1).astype(BF16)
        q, k, v, dq, dk, dv = _in_proj(
            x2d, row(g_mix[layer]), w_in_b, row(g_q_lat[layer]),
            w_uq[layer][:, uq_cols].astype(BF16), row(g_kv_lat[layer]), w_ukv[layer].astype(BF16),
            tables, tm=512)
        out_mla = _mla_attn(q, k, v, batch, seq, tq=512)
        out_diff = _diff_attn(
            [row(lambda_q1[layer]), row(lambda_k1[layer]), row(lambda_q2[layer]), row(lambda_k2[layer])],
            dq, dk, dv, row(g_diff_sub[layer]), batch, seq, tq=512, lambda_init=lambda_init)
        xk, xv = _mem_kv(mem2d, row(g_mem[layer]), w_xk[layer].astype(BF16),
                         w_xv[layer].astype(BF16), m_len)
        x2d = _out_xattn(x2d, out_mla, out_diff, w_out[layer].astype(BF16), row(g_xattn[layer]),
                         w_xq[layer].astype(BF16), xk, xv, w_xo[layer].astype(BF16),
                         seq, m_len, tm=512)
        x2d = _ffn(x2d, row(g_ffn[layer]), w_gate[layer].astype(BF16), w_up[layer].astype(BF16),
                   w_down[layer].astype(BF16), row(g_final), tm=512, th=512,
                   final_norm=(layer == depth - 1))
    return x2d.reshape(batch, seq, d)
```

```python
import functools
import math

import numpy as np
import jax
import jax.numpy as jnp
from jax import lax
from jax.experimental import pallas as pl
from jax.experimental.pallas import tpu as pltpu

F32 = jnp.float32
BF16 = jnp.bfloat16

ROPE_THETA = 10000.0
NORM_EPS = 1e-6
DIFF_NORM_EPS = 1e-5
Q_LORA, KV_LORA, ROPE_DIM = 512, 256, 64
MLA_HEADS, NOPE_DIM, V_DIM = 8, 128, 128
DIFF_HEADS, DIFF_DIM = 4, 128
X_HEADS, X_DIM = 4, 128
LANES = 128
LOG2E = math.log2(math.e)
VMEM_LIMIT = 58 * 1024 * 1024

LAT_PAD = 1024
MLA_QK = 2 * LANES


def _rms(x, g, eps):
    return x * lax.rsqrt(jnp.mean(x * x, axis=-1, keepdims=True) + eps) * g


def _nt_dot(a, b):
    return lax.dot_general(a, b, (((1,), (1,)), ((), ())), preferred_element_type=F32)


def _softmax_parts(s):
    m = jnp.max(s, axis=-1, keepdims=True)
    e = jnp.exp2(s - m)
    return e, jnp.sum(e, axis=-1, keepdims=True)


def _rope_table_kernel(pos_ref, freq_ref, cosd_ref, sind_ref, cosm_ref, sinlo_ref, sinhi_ref):
    ang = pos_ref[...].astype(F32) * freq_ref[...]
    ct, st = jnp.cos(ang), jnp.sin(ang)
    ct_r, st_r = pltpu.roll(ct, 64, 1), pltpu.roll(st, 64, 1)
    lane = lax.broadcasted_iota(jnp.int32, ang.shape, 1)
    lo = lane < 64
    cosd_ref[...] = jnp.where(lo, ct, ct_r)
    sind_ref[...] = jnp.where(lo, -st, st_r)
    cosm_ref[...] = jnp.where(lo, ct_r, ct)
    sm = jnp.where(lo, st_r, st)
    first_half = (lane & 63) < 32
    sinlo_ref[...] = jnp.where(first_half, -sm, 0.0)
    sinhi_ref[...] = jnp.where(first_half, 0.0, sm)


def _rope_tables(positions, tile):
    n = positions.size
    inv_d = ROPE_THETA ** (-jnp.arange(0, DIFF_DIM, 2, dtype=F32) / DIFF_DIM)
    inv_m = ROPE_THETA ** (-jnp.arange(0, ROPE_DIM, 2, dtype=F32) / ROPE_DIM)
    freq = jnp.concatenate([inv_d, inv_m, inv_m]).reshape(1, LANES)
    tab = jax.ShapeDtypeStruct((n, LANES), F32)
    tspec = pl.BlockSpec((tile, LANES), lambda i: (i, 0))
    return pl.pallas_call(
        _rope_table_kernel,
        grid=(n // tile,),
        in_specs=[pl.BlockSpec((tile, 1), lambda i: (i, 0)), pl.BlockSpec((1, LANES), lambda i: (0, 0))],
        out_specs=[tspec] * 5,
        out_shape=[tab] * 5,
        name="rope_tables",
    )(positions.reshape(n, 1), freq)


def _in_proj_kernel(x_ref, gmix_ref, w_ref, gq_ref, wuq_ref, gkv_ref, wukv_ref,
                    cosd_ref, sind_ref, cosm_ref, sinlo_ref, sinhi_ref,
                    q_ref, k_ref, v_ref, dq_ref, dk_ref, dv_ref, h_ref, acc_ref,
                    *, mla_scale, diff_scale):
    j = pl.program_id(1)

    @pl.when(j == 0)
    def _():
        h_ref[...] = _rms(x_ref[...], gmix_ref[...], NORM_EPS).astype(BF16)

    acc_ref[...] = jnp.dot(h_ref[...], w_ref[...], preferred_element_type=F32)

    def rope64(t):
        return (t * cosm_ref[...] + pltpu.roll(t, 96, 1) * sinlo_ref[...]
                + pltpu.roll(t, 32, 1) * sinhi_ref[...])

    def rope128(t):
        return t * cosd_ref[...] + pltpu.roll(t, 64, 1) * sind_ref[...]

    @pl.when(j == 0)
    def _():
        c_q = _rms(acc_ref[:, :Q_LORA], gq_ref[...], NORM_EPS).astype(BF16)
        q = jnp.dot(c_q, wuq_ref[...], preferred_element_type=F32) * mla_scale
        nope_w = MLA_HEADS * NOPE_DIM
        q_rope = [rope64(q[:, nope_w + c * LANES: nope_w + (c + 1) * LANES]).astype(BF16)
                  for c in range(MLA_HEADS // 2)]
        c_kv = _rms(acc_ref[:, Q_LORA:Q_LORA + KV_LORA], gkv_ref[...], NORM_EPS).astype(BF16)
        kv = jnp.dot(c_kv, wukv_ref[...], preferred_element_type=F32)
        kr = rope64(acc_ref[:, Q_LORA + KV_LORA:Q_LORA + KV_LORA + LANES])
        kr_even, kr_odd = kr.astype(BF16), pltpu.roll(kr, 64, 1).astype(BF16)
        for h in range(MLA_HEADS):
            base = h * MLA_QK
            q_ref[:, base:base + NOPE_DIM] = q[:, h * NOPE_DIM:(h + 1) * NOPE_DIM].astype(BF16)
            q_ref[:, base + NOPE_DIM:base + MLA_QK] = q_rope[h // 2]
            k_ref[:, base:base + NOPE_DIM] = kv[:, base:base + NOPE_DIM].astype(BF16)
            k_ref[:, base + NOPE_DIM:base + MLA_QK] = kr_even if h % 2 == 0 else kr_odd
            v_ref[:, h * V_DIM:(h + 1) * V_DIM] = kv[:, base + NOPE_DIM:base + MLA_QK].astype(BF16)

    @pl.when(j == 1)
    def _():
        for c in range(acc_ref.shape[1] // LANES):
            sl = slice(c * LANES, (c + 1) * LANES)
            dq_ref[:, sl] = (rope128(acc_ref[:, sl]) * diff_scale).astype(BF16)

    @pl.when(j == 2)
    def _():
        for c in range(acc_ref.shape[1] // LANES):
            sl = slice(c * LANES, (c + 1) * LANES)
            dk_ref[:, sl] = rope128(acc_ref[:, sl]).astype(BF16)

    @pl.when(j == 3)
    def _():
        dv_ref[...] = acc_ref[...].astype(BF16)


def _in_proj(x2d, g_mix, w_in_b, g_q, w_uq_b, g_kv, w_ukv_b, tables, tm):
    n, d = x2d.shape
    nj = w_in_b.shape[1] // LAT_PAD
    diff_w = 2 * DIFF_HEADS * DIFF_DIM
    row = lambda i, j: (i, 0)
    const = lambda i, j: (0, 0)
    kern = functools.partial(
        _in_proj_kernel,
        mla_scale=(NOPE_DIM + ROPE_DIM) ** -0.5 * LOG2E,
        diff_scale=DIFF_DIM ** -0.5 * LOG2E)
    outs = [
        jax.ShapeDtypeStruct((n, MLA_HEADS * MLA_QK), BF16),
        jax.ShapeDtypeStruct((n, MLA_HEADS * MLA_QK), BF16),
        jax.ShapeDtypeStruct((n, MLA_HEADS * V_DIM), BF16),
        jax.ShapeDtypeStruct((n, diff_w), BF16),
        jax.ShapeDtypeStruct((n, diff_w), BF16),
        jax.ShapeDtypeStruct((n, diff_w), BF16),
    ]
    return pl.pallas_call(
        kern,
        grid=(n // tm, nj),
        in_specs=[
            pl.BlockSpec((tm, d), row),
            pl.BlockSpec((1, d), const),
            pl.BlockSpec((d, LAT_PAD), lambda i, j: (0, j)),
            pl.BlockSpec((1, Q_LORA), const),
            pl.BlockSpec(w_uq_b.shape, const),
            pl.BlockSpec((1, KV_LORA), const),
            pl.BlockSpec(w_ukv_b.shape, const),
        ] + [pl.BlockSpec((tm, LANES), row)] * 5,
        out_specs=[pl.BlockSpec((tm, o.shape[1]), row) for o in outs],
        out_shape=outs,
        scratch_shapes=[pltpu.VMEM((tm, d), BF16), pltpu.VMEM((tm, LAT_PAD), F32)],
        compiler_params=pltpu.CompilerParams(
            dimension_semantics=("parallel", "arbitrary"), vmem_limit_bytes=VMEM_LIMIT),
        name="in_proj",
    )(x2d, g_mix, w_in_b, g_q, w_uq_b, g_kv, w_ukv_b, *tables)


def _mla_attn_kernel(q_ref, k_ref, v_ref, o_ref, s0_ref, s1_ref, p0_ref, p1_ref, *, tq):
    nq = q_ref.shape[0] // tq
    s_bufs, p_bufs, inv_l = (s0_ref, s1_ref), (p0_ref, p1_ref), {}
    rows = lambda t: slice(t * tq, (t + 1) * tq)

    def scores(t):
        s_bufs[t % 2][...] = _nt_dot(q_ref[rows(t), :], k_ref[...])

    def softmax(t):
        e, l = _softmax_parts(s_bufs[t % 2][...])
        p_bufs[t % 2][...] = e.astype(BF16)
        inv_l[t] = 1.0 / l

    def pv(t):
        o = jnp.dot(p_bufs[t % 2][...], v_ref[...], preferred_element_type=F32)
        o_ref[rows(t), :] = (o * inv_l.pop(t)).astype(o_ref.dtype)

    scores(0)
    for t in range(nq):
        if t + 1 < nq:
            scores(t + 1)
        softmax(t)
        if t >= 1:
            pv(t - 1)
    pv(nq - 1)


def _mla_attn(q, k, v, batch, seq, tq):
    blk = lambda w: pl.BlockSpec((seq, w), lambda b, h: (b, h))
    return pl.pallas_call(
        functools.partial(_mla_attn_kernel, tq=tq),
        grid=(batch, MLA_HEADS),
        in_specs=[blk(MLA_QK), blk(MLA_QK), blk(V_DIM)],
        out_specs=blk(V_DIM),
        out_shape=jax.ShapeDtypeStruct(v.shape, BF16),
        scratch_shapes=[pltpu.VMEM((tq, seq), F32)] * 2 + [pltpu.VMEM((tq, seq), BF16)] * 2,
        compiler_params=pltpu.CompilerParams(
            dimension_semantics=("parallel", "parallel"), vmem_limit_bytes=VMEM_LIMIT),
        name="mla_attn",
    )(q, k, v)


def _diff_attn_kernel(lq1_ref, lk1_ref, lq2_ref, lk2_ref, q_ref, k_ref, v_ref, g_ref, o_ref,
                      sa0_ref, sa1_ref, sb0_ref, sb1_ref, p0_ref, p1_ref, *, tq, lambda_init):
    lam = (jnp.exp(jnp.sum(lq1_ref[...] * lk1_ref[...], axis=-1, keepdims=True))
           - jnp.exp(jnp.sum(lq2_ref[...] * lk2_ref[...], axis=-1, keepdims=True))
           + lambda_init)
    nq = q_ref.shape[0] // tq
    sa_bufs, sb_bufs, p_bufs, inv_l1 = (sa0_ref, sa1_ref), (sb0_ref, sb1_ref), (p0_ref, p1_ref), {}
    rows = lambda t: slice(t * tq, (t + 1) * tq)

    def scores(t):
        sa_bufs[t % 2][...] = _nt_dot(q_ref[rows(t), :DIFF_DIM], k_ref[:, :DIFF_DIM])
        sb_bufs[t % 2][...] = _nt_dot(q_ref[rows(t), DIFF_DIM:], k_ref[:, DIFF_DIM:])

    def softmax(t):
        e1, l1 = _softmax_parts(sa_bufs[t % 2][...])
        e2, l2 = _softmax_parts(sb_bufs[t % 2][...])
        p_bufs[t % 2][...] = (e1 - e2 * (lam * l1 / l2)).astype(BF16)
        inv_l1[t] = 1.0 / l1

    def pv(t):
        o = jnp.dot(p_bufs[t % 2][...], v_ref[...], preferred_element_type=F32) * inv_l1.pop(t)
        o = _rms(o, g_ref[...], DIFF_NORM_EPS) * (1.0 - lambda_init)
        o_ref[rows(t), :] = o.astype(o_ref.dtype)

    scores(0)
    for t in range(nq):
        if t + 1 < nq:
            scores(t + 1)
        softmax(t)
        if t >= 1:
            pv(t - 1)
    pv(nq - 1)


def _diff_attn(lams, dq, dk, dv, g_sub, batch, seq, tq, lambda_init):
    w = 2 * DIFF_DIM
    blk = pl.BlockSpec((seq, w), lambda b, h: (b, h))
    vec = lambda width: pl.BlockSpec((1, width), lambda b, h: (0, 0))
    return pl.pallas_call(
        functools.partial(_diff_attn_kernel, tq=tq, lambda_init=lambda_init),
        grid=(batch, DIFF_HEADS),
        in_specs=[vec(DIFF_DIM)] * 4 + [blk, blk, blk, vec(w)],
        out_specs=blk,
        out_shape=jax.ShapeDtypeStruct(dv.shape, BF16),
        scratch_shapes=[pltpu.VMEM((tq, seq), F32)] * 4 + [pltpu.VMEM((tq, seq), BF16)] * 2,
        compiler_params=pltpu.CompilerParams(
            dimension_semantics=("parallel", "parallel"), vmem_limit_bytes=VMEM_LIMIT),
        name="diff_attn",
    )(*lams, dq, dk, dv, g_sub)


def _mem_kv_kernel(mem_ref, g_ref, wk_ref, wv_ref, k_ref, v_ref):
    hm = _rms(mem_ref[...], g_ref[...], NORM_EPS).astype(BF16)
    k_ref[...] = jnp.dot(hm, wk_ref[...], preferred_element_type=F32).astype(BF16)
    v_ref[...] = jnp.dot(hm, wv_ref[...], preferred_element_type=F32).astype(BF16)


def _mem_kv(mem2d, g_mem, w_xk_b, w_xv_b, m_len):
    n, d = mem2d.shape
    xw = w_xk_b.shape[1]
    const = lambda b: (0, 0)
    out = jax.ShapeDtypeStruct((n, xw), BF16)
    return pl.pallas_call(
        _mem_kv_kernel,
        grid=(n // m_len,),
        in_specs=[pl.BlockSpec((m_len, d), lambda b: (b, 0)), pl.BlockSpec((1, d), const),
                  pl.BlockSpec((d, xw), const), pl.BlockSpec((d, xw), const)],
        out_specs=[pl.BlockSpec((m_len, xw), lambda b: (b, 0))] * 2,
        out_shape=[out, out],
        compiler_params=pltpu.CompilerParams(
            dimension_semantics=("parallel",), vmem_limit_bytes=VMEM_LIMIT),
        name="mem_kv",
    )(mem2d, g_mem, w_xk_b, w_xv_b)


def _out_xattn_kernel(x_ref, a_ref, b_ref, wo_ref, gx_ref, wxq_ref, xk_ref, xv_ref, wxo_ref, o_ref,
                      *, x_scale):
    half = a_ref.shape[1]
    x1 = (x_ref[...]
          + jnp.dot(a_ref[...], wo_ref[:half, :], preferred_element_type=F32)
          + jnp.dot(b_ref[...], wo_ref[half:, :], preferred_element_type=F32))
    hx = _rms(x1, gx_ref[...], NORM_EPS).astype(BF16)
    xq = (jnp.dot(hx, wxq_ref[...], preferred_element_type=F32) * x_scale).astype(BF16)
    heads = []
    for h in range(X_HEADS):
        sl = slice(h * X_DIM, (h + 1) * X_DIM)
        e, l = _softmax_parts(_nt_dot(xq[:, sl], xk_ref[:, sl]))
        o = jnp.dot(e.astype(BF16), xv_ref[:, sl], preferred_element_type=F32)
        heads.append((o * (1.0 / l)).astype(BF16))
    xo = jnp.concatenate(heads, axis=-1)
    o_ref[...] = x1 + jnp.dot(xo, wxo_ref[...], preferred_element_type=F32)


def _out_xattn(x2d, a, b, w_out_b, g_x, w_xq_b, xk, xv, w_xo_b, seq, m_len, tm):
    n, d = x2d.shape
    xw = w_xq_b.shape[1]
    per_b = seq // tm
    row = lambda i: (i, 0)
    const = lambda i: (0, 0)
    mem = lambda i: (i // per_b, 0)
    return pl.pallas_call(
        functools.partial(_out_xattn_kernel, x_scale=X_DIM ** -0.5 * LOG2E),
        grid=(n // tm,),
        in_specs=[
            pl.BlockSpec((tm, d), row),
            pl.BlockSpec((tm, a.shape[1]), row),
            pl.BlockSpec((tm, b.shape[1]), row),
            pl.BlockSpec(w_out_b.shape, const),
            pl.BlockSpec((1, d), const),
            pl.BlockSpec(w_xq_b.shape, const),
            pl.BlockSpec((m_len, xw), mem),
            pl.BlockSpec((m_len, xw), mem),
            pl.BlockSpec(w_xo_b.shape, const),
        ],
        out_specs=pl.BlockSpec((tm, d), row),
        out_shape=jax.ShapeDtypeStruct((n, d), F32),
        compiler_params=pltpu.CompilerParams(
            dimension_semantics=("parallel",), vmem_limit_bytes=VMEM_LIMIT),
        name="out_xattn",
    )(x2d, a, b, w_out_b, g_x, w_xq_b, xk, xv, w_xo_b)


def _ffn_kernel(x_ref, g_ref, wg_ref, wu_ref, wd_ref, gf_ref, o_ref, h_ref, acc_ref, *, final_norm):
    j = pl.program_id(1)

    @pl.when(j == 0)
    def _():
        x = x_ref[...]
        h_ref[...] = _rms(x, g_ref[...], NORM_EPS).astype(BF16)
        acc_ref[...] = x

    h = h_ref[...]
    gate = jnp.dot(h, wg_ref[...], preferred_element_type=F32)
    up = jnp.dot(h, wu_ref[...], preferred_element_type=F32)
    act = (gate * jax.nn.sigmoid(gate) * up).astype(BF16)
    acc_ref[...] += jnp.dot(act, wd_ref[...], preferred_element_type=F32)

    @pl.when(j == pl.num_programs(1) - 1)
    def _():
        y = acc_ref[...]
        o_ref[...] = _rms(y, gf_ref[...], NORM_EPS) if final_norm else y


def _ffn(x2d, g_ffn, w_gate_b, w_up_b, w_down_b, g_final, tm, th, final_norm):
    n, d = x2d.shape
    hidden = w_gate_b.shape[1]
    row = lambda i, j: (i, 0)
    const = lambda i, j: (0, 0)
    return pl.pallas_call(
        functools.partial(_ffn_kernel, final_norm=final_norm),
        grid=(n // tm, hidden // th),
        in_specs=[
            pl.BlockSpec((tm, d), row),
            pl.BlockSpec((1, d), const),
            pl.BlockSpec((d, th), lambda i, j: (0, j)),
            pl.BlockSpec((d, th), lambda i, j: (0, j)),
            pl.BlockSpec((th, d), lambda i, j: (j, 0)),
            pl.BlockSpec((1, d), const),
        ],
        out_specs=pl.BlockSpec((tm, d), row),
        out_shape=jax.ShapeDtypeStruct((n, d), F32),
        scratch_shapes=[pltpu.VMEM((tm, d), BF16), pltpu.VMEM((tm, d), F32)],
        compiler_params=pltpu.CompilerParams(
            dimension_semantics=("parallel", "arbitrary"), vmem_limit_bytes=VMEM_LIMIT),
        name="ffn",
    )(x2d, g_ffn, w_gate_b, w_up_b, w_down_b, g_final)


def kernel(x, mem, positions, g_mix, w_in, g_q_lat, w_uq, g_kv_lat, w_ukv, lambda_q1, lambda_k1,
           lambda_q2, lambda_k2, g_diff_sub, w_out, g_xattn, g_mem, w_xq, w_xk, w_xv, w_xo,
           g_ffn, w_gate, w_up, w_down, g_final):
    batch, seq, d = x.shape
    m_len = mem.shape[1]
    depth = w_in.shape[0]
    n = batch * seq
    lat_w = Q_LORA + KV_LORA + ROPE_DIM

    tables = _rope_tables(positions, tile=2048)

    head_w = NOPE_DIM + ROPE_DIM
    uq_cols = np.concatenate(
        [h * head_w + np.arange(NOPE_DIM) for h in range(MLA_HEADS)]
        + [h * head_w + NOPE_DIM + np.arange(ROPE_DIM) for h in range(MLA_HEADS)])

    x2d = x.reshape(n, d)
    mem2d = mem.reshape(batch * m_len, d)
    row = lambda v: v.reshape(1, -1)

    for layer in range(depth):
        lambda_init = 0.8 - 0.6 * math.exp(-0.3 * layer)
        w_in_l = w_in[layer]
        w_in_b = jnp.concatenate(
            [w_in_l[:, :lat_w], jnp.zeros((d, LAT_PAD - lat_w), F32), w_in_l[:, lat_w:]],
            axis=1).astype(BF16)
        q, k, v, dq, dk, dv = _in_proj(
            x2d, row(g_mix[layer]), w_in_b, row(g_q_lat[layer]),
            w_uq[layer][:, uq_cols].astype(BF16), row(g_kv_lat[layer]), w_ukv[layer].astype(BF16),
            tables, tm=512)
        out_mla = _mla_attn(q, k, v, batch, seq, tq=512)
        out_diff = _diff_attn(
            [row(lambda_q1[layer]), row(lambda_k1[layer]), row(lambda_q2[layer]), row(lambda_k2[layer])],
            dq, dk, dv, row(g_diff_sub[layer]), batch, seq, tq=512, lambda_init=lambda_init)
        xk, xv = _mem_kv(mem2d, row(g_mem[layer]), w_xk[layer].astype(BF16),
                         w_xv[layer].astype(BF16), m_len)
        x2d = _out_xattn(x2d, out_mla, out_diff, w_out[layer].astype(BF16), row(g_xattn[layer]),
                         w_xq[layer].astype(BF16), xk, xv, w_xo[layer].astype(BF16),
                         seq, m_len, tm=512)
        x2d = _ffn(x2d, row(g_ffn[layer]), w_gate[layer].astype(BF16), w_up[layer].astype(BF16),
                   w_down[layer].astype(BF16), row(g_final), tm=512, th=512,
                   final_norm=(layer == depth - 1))
    return x2d.reshape(batch, seq, d)
```

```python
import functools
import math

import numpy as np
import jax
import jax.numpy as jnp
from jax import lax
from jax.experimental import pallas as pl
from jax.experimental.pallas import tpu as pltpu

F32 = jnp.float32
BF16 = jnp.bfloat16

ROPE_THETA = 10000.0
NORM_EPS = 1e-6
DIFF_NORM_EPS = 1e-5
Q_LORA, KV_LORA, ROPE_DIM = 512, 256, 64
MLA_HEADS, NOPE_DIM, V_DIM = 8, 128, 128
DIFF_HEADS, DIFF_DIM = 4, 128
X_HEADS, X_DIM = 4, 128
LANES = 128
LOG2E = math.log2(math.e)
VMEM_LIMIT = 58 * 1024 * 1024

LAT_PAD = 1024
MLA_QK = 2 * LANES


def _rms(x, g, eps):
    return x * lax.rsqrt(jnp.mean(x * x, axis=-1, keepdims=True) + eps) * g


def _nt_dot(a, b):
    return lax.dot_general(a, b, (((1,), (1,)), ((), ())), preferred_element_type=F32)


def _softmax_parts(s):
    m = jnp.max(s, axis=-1, keepdims=True)
    e = jnp.exp2(s - m)
    return e, jnp.sum(e, axis=-1, keepdims=True)


def _rope_table_kernel(pos_ref, freq_ref, cosd_ref, sind_ref, cosm_ref, sinlo_ref, sinhi_ref):
    ang = pos_ref[...].astype(F32) * freq_ref[...]
    ct, st = jnp.cos(ang), jnp.sin(ang)
    ct_r, st_r = pltpu.roll(ct, 64, 1), pltpu.roll(st, 64, 1)
    lane = lax.broadcasted_iota(jnp.int32, ang.shape, 1)
    lo = lane < 64
    cosd_ref[...] = jnp.where(lo, ct, ct_r)
    sind_ref[...] = jnp.where(lo, -st, st_r)
    cosm_ref[...] = jnp.where(lo, ct_r, ct)
    sm = jnp.where(lo, st_r, st)
    first_half = (lane & 63) < 32
    sinlo_ref[...] = jnp.where(first_half, -sm, 0.0)
    sinhi_ref[...] = jnp.where(first_half, 0.0, sm)


def _rope_tables(positions, tile):
    n = positions.size
    inv_d = ROPE_THETA ** (-jnp.arange(0, DIFF_DIM, 2, dtype=F32) / DIFF_DIM)
    inv_m = ROPE_THETA ** (-jnp.arange(0, ROPE_DIM, 2, dtype=F32) / ROPE_DIM)
    freq = jnp.concatenate([inv_d, inv_m, inv_m]).reshape(1, LANES)
    tab = jax.ShapeDtypeStruct((n, LANES), F32)
    tspec = pl.BlockSpec((tile, LANES), lambda i: (i, 0))
    return pl.pallas_call(
        _rope_table_kernel,
        grid=(n // tile,),
        in_specs=[pl.BlockSpec((tile, 1), lambda i: (i, 0)), pl.BlockSpec((1, LANES), lambda i: (0, 0))],
        out_specs=[tspec] * 5,
        out_shape=[tab] * 5,
        name="rope_tables",
    )(positions.reshape(n, 1), freq)


def _in_proj_kernel(x_ref, gmix_ref, w_ref, gq_ref, wuq_ref, gkv_ref, wukv_ref,
                    cosd_ref, sind_ref, cosm_ref, sinlo_ref, sinhi_ref,
                    q_ref, k_ref, v_ref, dq_ref, dk_ref, dv_ref, h_ref,
                    *, mla_scale, diff_scale):
    j = pl.program_id(1)

    def project():
        return jnp.dot(h_ref[...], w_ref[...], preferred_element_type=F32)

    def rope64(t):
        return (t * cosm_ref[...] + pltpu.roll(t, 96, 1) * sinlo_ref[...]
                + pltpu.roll(t, 32, 1) * sinhi_ref[...])

    def rope128(t):
        return t * cosd_ref[...] + pltpu.roll(t, 64, 1) * sind_ref[...]

    @pl.when(j == 0)
    def _():
        h_ref[...] = _rms(x_ref[...], gmix_ref[...], NORM_EPS).astype(BF16)
        acc = project()
        c_q = _rms(acc[:, :Q_LORA], gq_ref[...], NORM_EPS).astype(BF16)
        q = jnp.dot(c_q, wuq_ref[...], preferred_element_type=F32) * mla_scale
        nope_w = MLA_HEADS * NOPE_DIM
        q_rope = [rope64(q[:, nope_w + c * LANES: nope_w + (c + 1) * LANES]).astype(BF16)
                  for c in range(MLA_HEADS // 2)]
        c_kv = _rms(acc[:, Q_LORA:Q_LORA + KV_LORA], gkv_ref[...], NORM_EPS).astype(BF16)
        kv = jnp.dot(c_kv, wukv_ref[...], preferred_element_type=F32)
        kr = rope64(acc[:, Q_LORA + KV_LORA:Q_LORA + KV_LORA + LANES])
        kr_even, kr_odd = kr.astype(BF16), pltpu.roll(kr, 64, 1).astype(BF16)
        for h in range(MLA_HEADS):
            base = h * MLA_QK
            q_ref[:, base:base + NOPE_DIM] = q[:, h * NOPE_DIM:(h + 1) * NOPE_DIM].astype(BF16)
            q_ref[:, base + NOPE_DIM:base + MLA_QK] = q_rope[h // 2]
            k_ref[:, base:base + NOPE_DIM] = kv[:, base:base + NOPE_DIM].astype(BF16)
            k_ref[:, base + NOPE_DIM:base + MLA_QK] = kr_even if h % 2 == 0 else kr_odd
            v_ref[:, h * V_DIM:(h + 1) * V_DIM] = kv[:, base + NOPE_DIM:base + MLA_QK].astype(BF16)

    @pl.when(j == 1)
    def _():
        acc = project()
        for c in range(acc.shape[1] // LANES):
            sl = slice(c * LANES, (c + 1) * LANES)
            dq_ref[:, sl] = (rope128(acc[:, sl]) * diff_scale).astype(BF16)

    @pl.when(j == 2)
    def _():
        acc = project()
        for c in range(acc.shape[1] // LANES):
            sl = slice(c * LANES, (c + 1) * LANES)
            dk_ref[:, sl] = rope128(acc[:, sl]).astype(BF16)

    @pl.when(j == 3)
    def _():
        dv_ref[...] = project().astype(BF16)


def _in_proj(x2d, g_mix, w_in_b, g_q, w_uq_b, g_kv, w_ukv_b, tables, tm):
    n, d = x2d.shape
    nj = w_in_b.shape[1] // LAT_PAD
    diff_w = 2 * DIFF_HEADS * DIFF_DIM
    row = lambda i, j: (i, 0)
    const = lambda i, j: (0, 0)
    kern = functools.partial(
        _in_proj_kernel,
        mla_scale=(NOPE_DIM + ROPE_DIM) ** -0.5 * LOG2E,
        diff_scale=DIFF_DIM ** -0.5 * LOG2E)
    outs = [
        jax.ShapeDtypeStruct((n, MLA_HEADS * MLA_QK), BF16),
        jax.ShapeDtypeStruct((n, MLA_HEADS * MLA_QK), BF16),
        jax.ShapeDtypeStruct((n, MLA_HEADS * V_DIM), BF16),
        jax.ShapeDtypeStruct((n, diff_w), BF16),
        jax.ShapeDtypeStruct((n, diff_w), BF16),
        jax.ShapeDtypeStruct((n, diff_w), BF16),
    ]
    return pl.pallas_call(
        kern,
        grid=(n // tm, nj),
        in_specs=[
            pl.BlockSpec((tm, d), row),
            pl.BlockSpec((1, d), const),
            pl.BlockSpec((d, LAT_PAD), lambda i, j: (0, j)),
            pl.BlockSpec((1, Q_LORA), const),
            pl.BlockSpec(w_uq_b.shape, const),
            pl.BlockSpec((1, KV_LORA), const),
            pl.BlockSpec(w_ukv_b.shape, const),
        ] + [pl.BlockSpec((tm, LANES), row)] * 5,
        out_specs=[pl.BlockSpec((tm, o.shape[1]), row) for o in outs],
        out_shape=outs,
        scratch_shapes=[pltpu.VMEM((tm, d), BF16)],
        compiler_params=pltpu.CompilerParams(
            dimension_semantics=("parallel", "arbitrary"), vmem_limit_bytes=VMEM_LIMIT),
        name="in_proj",
    )(x2d, g_mix, w_in_b, g_q, w_uq_b, g_kv, w_ukv_b, *tables)


def _mla_attn_kernel(q_ref, k_ref, v_ref, o_ref, s0_ref, s1_ref, p0_ref, p1_ref, *, tq):
    nq = q_ref.shape[0] // tq
    s_bufs, p_bufs, inv_l = (s0_ref, s1_ref), (p0_ref, p1_ref), {}
    rows = lambda t: slice(t * tq, (t + 1) * tq)

    def scores(t):
        s_bufs[t % 2][...] = _nt_dot(q_ref[rows(t), :], k_ref[...])

    def softmax(t):
        e, l = _softmax_parts(s_bufs[t % 2][...])
        p_bufs[t % 2][...] = e.astype(BF16)
        inv_l[t] = 1.0 / l

    def pv(t):
        o = jnp.dot(p_bufs[t % 2][...], v_ref[...], preferred_element_type=F32)
        o_ref[rows(t), :] = (o * inv_l.pop(t)).astype(o_ref.dtype)

    scores(0)
    for t in range(nq):
        if t + 1 < nq:
            scores(t + 1)
        softmax(t)
        if t >= 1:
            pv(t - 1)
    pv(nq - 1)


def _mla_attn(q, k, v, batch, seq, tq):
    blk = lambda w: pl.BlockSpec((seq, w), lambda b, h: (b, h))
    return pl.pallas_call(
        functools.partial(_mla_attn_kernel, tq=tq),
        grid=(batch, MLA_HEADS),
        in_specs=[blk(MLA_QK), blk(MLA_QK), blk(V_DIM)],
        out_specs=blk(V_DIM),
        out_shape=jax.ShapeDtypeStruct(v.shape, BF16),
        scratch_shapes=[pltpu.VMEM((tq, seq), F32)] * 2 + [pltpu.VMEM((tq, seq), BF16)] * 2,
        compiler_params=pltpu.CompilerParams(
            dimension_semantics=("parallel", "parallel"), vmem_limit_bytes=VMEM_LIMIT),
        name="mla_attn",
    )(q, k, v)


def _diff_attn_kernel(lq1_ref, lk1_ref, lq2_ref, lk2_ref, q_ref, k_ref, v_ref, g_ref, o_ref,
                      sa0_ref, sa1_ref, sb0_ref, sb1_ref, p0_ref, p1_ref, *, tq, lambda_init):
    lam = (jnp.exp(jnp.sum(lq1_ref[...] * lk1_ref[...], axis=-1, keepdims=True))
           - jnp.exp(jnp.sum(lq2_ref[...] * lk2_ref[...], axis=-1, keepdims=True))
           + lambda_init)
    nq = q_ref.shape[0] // tq
    sa_bufs, sb_bufs, p_bufs, inv_l1 = (sa0_ref, sa1_ref), (sb0_ref, sb1_ref), (p0_ref, p1_ref), {}
    rows = lambda t: slice(t * tq, (t + 1) * tq)

    def scores(t):
        sa_bufs[t % 2][...] = _nt_dot(q_ref[rows(t), :DIFF_DIM], k_ref[:, :DIFF_DIM])
        sb_bufs[t % 2][...] = _nt_dot(q_ref[rows(t), DIFF_DIM:], k_ref[:, DIFF_DIM:])

    def softmax(t):
        e1, l1 = _softmax_parts(sa_bufs[t % 2][...])
        e2, l2 = _softmax_parts(sb_bufs[t % 2][...])
        p_bufs[t % 2][...] = (e1 - e2 * (lam * l1 / l2)).astype(BF16)
        inv_l1[t] = 1.0 / l1

    def pv(t):
        o = jnp.dot(p_bufs[t % 2][...], v_ref[...], preferred_element_type=F32) * inv_l1.pop(t)
        o = _rms(o, g_ref[...], DIFF_NORM_EPS) * (1.0 - lambda_init)
        o_ref[rows(t), :] = o.astype(o_ref.dtype)

    scores(0)
    for t in range(nq):
        if t + 1 < nq:
            scores(t + 1)
        softmax(t)
        if t >= 1:
            pv(t - 1)
    pv(nq - 1)


def _diff_attn(lams, dq, dk, dv, g_sub, batch, seq, tq, lambda_init):
    w = 2 * DIFF_DIM
    blk = pl.BlockSpec((seq, w), lambda b, h: (b, h))
    vec = lambda width: pl.BlockSpec((1, width), lambda b, h: (0, 0))
    return pl.pallas_call(
        functools.partial(_diff_attn_kernel, tq=tq, lambda_init=lambda_init),
        grid=(batch, DIFF_HEADS),
        in_specs=[vec(DIFF_DIM)] * 4 + [blk, blk, blk, vec(w)],
        out_specs=blk,
        out_shape=jax.ShapeDtypeStruct(dv.shape, BF16),
        scratch_shapes=[pltpu.VMEM((tq, seq), F32)] * 4 + [pltpu.VMEM((tq, seq), BF16)] * 2,
        compiler_params=pltpu.CompilerParams(
            dimension_semantics=("parallel", "parallel"), vmem_limit_bytes=VMEM_LIMIT),
        name="diff_attn",
    )(*lams, dq, dk, dv, g_sub)


def _mem_kv_kernel(mem_ref, g_ref, wk_ref, wv_ref, k_ref, v_ref):
    hm = _rms(mem_ref[...], g_ref[...], NORM_EPS).astype(BF16)
    k_ref[...] = jnp.dot(hm, wk_ref[...], preferred_element_type=F32).astype(BF16)
    v_ref[...] = jnp.dot(hm, wv_ref[...], preferred_element_type=F32).astype(BF16)


def _mem_kv(mem2d, g_mem, w_xk_b, w_xv_b, m_len):
    n, d = mem2d.shape
    xw = w_xk_b.shape[1]
    const = lambda b: (0, 0)
    out = jax.ShapeDtypeStruct((n, xw), BF16)
    return pl.pallas_call(
        _mem_kv_kernel,
        grid=(n // m_len,),
        in_specs=[pl.BlockSpec((m_len, d), lambda b: (b, 0)), pl.BlockSpec((1, d), const),
                  pl.BlockSpec((d, xw), const), pl.BlockSpec((d, xw), const)],
        out_specs=[pl.BlockSpec((m_len, xw), lambda b: (b, 0))] * 2,
        out_shape=[out, out],
        compiler_params=pltpu.CompilerParams(
            dimension_semantics=("parallel",), vmem_limit_bytes=VMEM_LIMIT),
        name="mem_kv",
    )(mem2d, g_mem, w_xk_b, w_xv_b)


def _out_xattn_kernel(x_ref, a_ref, b_ref, wo_ref, gx_ref, wxq_ref, xk_ref, xv_ref, wxo_ref, o_ref,
                      *, x_scale):
    half = a_ref.shape[1]
    x1 = (x_ref[...]
          + jnp.dot(a_ref[...], wo_ref[:half, :], preferred_element_type=F32)
          + jnp.dot(b_ref[...], wo_ref[half:, :], preferred_element_type=F32))
    hx = _rms(x1, gx_ref[...], NORM_EPS).astype(BF16)
    xq = (jnp.dot(hx, wxq_ref[...], preferred_element_type=F32) * x_scale).astype(BF16)
    heads = []
    for h in range(X_HEADS):
        sl = slice(h * X_DIM, (h + 1) * X_DIM)
        e, l = _softmax_parts(_nt_dot(xq[:, sl], xk_ref[:, sl]))
        o = jnp.dot(e.astype(BF16), xv_ref[:, sl], preferred_element_type=F32)
        heads.append((o * (1.0 / l)).astype(BF16))
    xo = jnp.concatenate(heads, axis=-1)
    o_ref[...] = x1 + jnp.dot(xo, wxo_ref[...], preferred_element_type=F32)


def _out_xattn(x2d, a, b, w_out_b, g_x, w_xq_b, xk, xv, w_xo_b, seq, m_len, tm):
    n, d = x2d.shape
    xw = w_xq_b.shape[1]
    per_b = seq // tm
    row = lambda i: (i, 0)
    const = lambda i: (0, 0)
    mem = lambda i: (i // per_b, 0)
    return pl.pallas_call(
        functools.partial(_out_xattn_kernel, x_scale=X_DIM ** -0.5 * LOG2E),
        grid=(n // tm,),
        in_specs=[
            pl.BlockSpec((tm, d), row),
            pl.BlockSpec((tm, a.shape[1]), row),
            pl.BlockSpec((tm, b.shape[1]), row),
            pl.BlockSpec(w_out_b.shape, const),
            pl.BlockSpec((1, d), const),
            pl.BlockSpec(w_xq_b.shape, const),
            pl.BlockSpec((m_len, xw), mem),
            pl.BlockSpec((m_len, xw), mem),
            pl.BlockSpec(w_xo_b.shape, const),
        ],
        out_specs=pl.BlockSpec((tm, d), row),
        out_shape=jax.ShapeDtypeStruct((n, d), F32),
        compiler_params=pltpu.CompilerParams(
            dimension_semantics=("parallel",), vmem_limit_bytes=VMEM_LIMIT),
        name="out_xattn",
    )(x2d, a, b, w_out_b, g_x, w_xq_b, xk, xv, w_xo_b)


def _ffn_kernel(x_ref, g_ref, wg_ref, wu_ref, wd_ref, gf_ref, o_ref, h_ref, acc_ref, *, final_norm):
    j = pl.program_id(1)

    @pl.when(j == 0)
    def _():
        x = x_ref[...]
        h_ref[...] = _rms(x, g_ref[...], NORM_EPS).astype(BF16)
        acc_ref[...] = x

    h = h_ref[...]
    gate = jnp.dot(h, wg_ref[...], preferred_element_type=F32)
    up = jnp.dot(h, wu_ref[...], preferred_element_type=F32)
    act = (gate * jax.nn.sigmoid(gate) * up).astype(BF16)
    acc_ref[...] += jnp.dot(act, wd_ref[...], preferred_element_type=F32)

    @pl.when(j == pl.num_programs(1) - 1)
    def _():
        y = acc_ref[...]
        o_ref[...] = _rms(y, gf_ref[...], NORM_EPS) if final_norm else y


def _ffn(x2d, g_ffn, w_gate_b, w_up_b, w_down_b, g_final, tm, th, final_norm):
    n, d = x2d.shape
    hidden = w_gate_b.shape[1]
    row = lambda i, j: (i, 0)
    const = lambda i, j: (0, 0)
    return pl.pallas_call(
        functools.partial(_ffn_kernel, final_norm=final_norm),
        grid=(n // tm, hidden // th),
        in_specs=[
            pl.BlockSpec((tm, d), row),
            pl.BlockSpec((1, d), const),
            pl.BlockSpec((d, th), lambda i, j: (0, j)),
            pl.BlockSpec((d, th), lambda i, j: (0, j)),
            pl.BlockSpec((th, d), lambda i, j: (j, 0)),
            pl.BlockSpec((1, d), const),
        ],
        out_specs=pl.BlockSpec((tm, d), row),
        out_shape=jax.ShapeDtypeStruct((n, d), F32),
        scratch_shapes=[pltpu.VMEM((tm, d), BF16), pltpu.VMEM((tm, d), F32)],
        compiler_params=pltpu.CompilerParams(
            dimension_semantics=("parallel", "arbitrary"), vmem_limit_bytes=VMEM_LIMIT),
        name="ffn",
    )(x2d, g_ffn, w_gate_b, w_up_b, w_down_b, g_final)


def kernel(x, mem, positions, g_mix, w_in, g_q_lat, w_uq, g_kv_lat, w_ukv, lambda_q1, lambda_k1,
           lambda_q2, lambda_k2, g_diff_sub, w_out, g_xattn, g_mem, w_xq, w_xk, w_xv, w_xo,
           g_ffn, w_gate, w_up, w_down, g_final):
    batch, seq, d = x.shape
    m_len = mem.shape[1]
    depth = w_in.shape[0]
    n = batch * seq
    lat_w = Q_LORA + KV_LORA + ROPE_DIM

    tables = _rope_tables(positions, tile=2048)

    head_w = NOPE_DIM + ROPE_DIM
    uq_cols = np.concatenate(
        [h * head_w + np.arange(NOPE_DIM) for h in range(MLA_HEADS)]
        + [h * head_w + NOPE_DIM + np.arange(ROPE_DIM) for h in range(MLA_HEADS)])

    x2d = x.reshape(n, d)
    mem2d = mem.reshape(batch * m_len, d)
    row = lambda v: v.reshape(1, -1)

    for layer in range(depth):
        lambda_init = 0.8 - 0.6 * math.exp(-0.3 * layer)
        w_in_l = w_in[layer]
        w_in_b = jnp.zeros((d, LAT_PAD + w_in_l.shape[1] - lat_w), BF16)
        w_in_b = lax.dynamic_update_slice(w_in_b, w_in_l[:, :lat_w].astype(BF16), (0, 0))
        w_in_b = lax.dynamic_update_slice(w_in_b, w_in_l[:, lat_w:].astype(BF16), (0, LAT_PAD))
        q, k, v, dq, dk, dv = _in_proj(
            x2d, row(g_mix[layer]), w_in_b, row(g_q_lat[layer]),
            w_uq[layer][:, uq_cols].astype(BF16), row(g_kv_lat[layer]), w_ukv[layer].astype(BF16),
            tables, tm=512)
        out_mla = _mla_attn(q, k, v, batch, seq, tq=512)
        out_diff = _diff_attn(
            [row(lambda_q1[layer]), row(lambda_k1[layer]), row(lambda_q2[layer]), row(lambda_k2[layer])],
            dq, dk, dv, row(g_diff_sub[layer]), batch, seq, tq=512, lambda_init=lambda_init)
        xk, xv = _mem_kv(mem2d, row(g_mem[layer]), w_xk[layer].astype(BF16),
                         w_xv[layer].astype(BF16), m_len)
        x2d = _out_xattn(x2d, out_mla, out_diff, w_out[layer].astype(BF16), row(g_xattn[layer]),
                         w_xq[layer].astype(BF16), xk, xv, w_xo[layer].astype(BF16),
                         seq, m_len, tm=512)
        x2d = _ffn(x2d, row(g_ffn[layer]), w_gate[layer].astype(BF16), w_up[layer].astype(BF16),
                   w_down[layer].astype(BF16), row(g_final), tm=512, th=512,
                   final_norm=(layer == depth - 1))
    return x2d.reshape(batch, seq, d)
```

```python
import functools
import math

import numpy as np
import jax
import jax.numpy as jnp
from jax import lax
from jax.experimental import pallas as pl
from jax.experimental.pallas import tpu as pltpu

F32 = jnp.float32
BF16 = jnp.bfloat16

ROPE_THETA = 10000.0
NORM_EPS = 1e-6
DIFF_NORM_EPS = 1e-5
Q_LORA, KV_LORA, ROPE_DIM = 512, 256, 64
MLA_HEADS, NOPE_DIM, V_DIM = 8, 128, 128
DIFF_HEADS, DIFF_DIM = 4, 128
X_HEADS, X_DIM = 4, 128
LANES = 128
LOG2E = math.log2(math.e)
VMEM_LIMIT = 58 * 1024 * 1024

LAT_PAD = 1024
MLA_QK = 2 * LANES


def _rms(x, g, eps):
    return x * lax.rsqrt(jnp.mean(x * x, axis=-1, keepdims=True) + eps) * g


def _nt_dot(a, b):
    return lax.dot_general(a, b, (((1,), (1,)), ((), ())), preferred_element_type=F32)


def _softmax_parts(s):
    m = jnp.max(s, axis=-1, keepdims=True)
    e = jnp.exp2(s - m)
    return e, jnp.sum(e, axis=-1, keepdims=True)


def _rope_table_kernel(pos_ref, freq_ref, cosd_ref, sind_ref, cosm_ref, sinlo_ref, sinhi_ref):
    ang = pos_ref[...].astype(F32) * freq_ref[...]
    ct, st = jnp.cos(ang), jnp.sin(ang)
    ct_r, st_r = pltpu.roll(ct, 64, 1), pltpu.roll(st, 64, 1)
    lane = lax.broadcasted_iota(jnp.int32, ang.shape, 1)
    lo = lane < 64
    cosd_ref[...] = jnp.where(lo, ct, ct_r)
    sind_ref[...] = jnp.where(lo, -st, st_r)
    cosm_ref[...] = jnp.where(lo, ct_r, ct)
    sm = jnp.where(lo, st_r, st)
    first_half = (lane & 63) < 32
    sinlo_ref[...] = jnp.where(first_half, -sm, 0.0)
    sinhi_ref[...] = jnp.where(first_half, 0.0, sm)


def _rope_tables(positions, tile):
    n = positions.size
    inv_d = ROPE_THETA ** (-jnp.arange(0, DIFF_DIM, 2, dtype=F32) / DIFF_DIM)
    inv_m = ROPE_THETA ** (-jnp.arange(0, ROPE_DIM, 2, dtype=F32) / ROPE_DIM)
    freq = jnp.concatenate([inv_d, inv_m, inv_m]).reshape(1, LANES)
    tab = jax.ShapeDtypeStruct((n, LANES), F32)
    tspec = pl.BlockSpec((tile, LANES), lambda i: (i, 0))
    return pl.pallas_call(
        _rope_table_kernel,
        grid=(n // tile,),
        in_specs=[pl.BlockSpec((tile, 1), lambda i: (i, 0)), pl.BlockSpec((1, LANES), lambda i: (0, 0))],
        out_specs=[tspec] * 5,
        out_shape=[tab] * 5,
        name="rope_tables",
    )(positions.reshape(n, 1), freq)


def _in_proj_kernel(x_ref, gmix_ref, w_ref, gq_ref, wuq_ref, gkv_ref, wukv_ref,
                    cosd_ref, sind_ref, cosm_ref, sinlo_ref, sinhi_ref,
                    q_ref, k_ref, v_ref, dq_ref, dk_ref, dv_ref, h_ref,
                    *, mla_scale, diff_scale):
    def project(group):
        cols = slice(group * LAT_PAD, (group + 1) * LAT_PAD)
        return jnp.dot(h_ref[...], w_ref[:, cols], preferred_element_type=F32)

    def rope64(t):
        return (t * cosm_ref[...] + pltpu.roll(t, 96, 1) * sinlo_ref[...]
                + pltpu.roll(t, 32, 1) * sinhi_ref[...])

    def rope128(t):
        return t * cosd_ref[...] + pltpu.roll(t, 64, 1) * sind_ref[...]

    h_ref[...] = _rms(x_ref[...], gmix_ref[...], NORM_EPS).astype(BF16)

    lat = project(0)
    c_q = _rms(lat[:, :Q_LORA], gq_ref[...], NORM_EPS).astype(BF16)
    q = jnp.dot(c_q, wuq_ref[...], preferred_element_type=F32) * mla_scale
    nope_w = MLA_HEADS * NOPE_DIM
    q_rope = [rope64(q[:, nope_w + c * LANES: nope_w + (c + 1) * LANES]).astype(BF16)
              for c in range(MLA_HEADS // 2)]
    c_kv = _rms(lat[:, Q_LORA:Q_LORA + KV_LORA], gkv_ref[...], NORM_EPS).astype(BF16)
    kv = jnp.dot(c_kv, wukv_ref[...], preferred_element_type=F32)
    kr = rope64(lat[:, Q_LORA + KV_LORA:Q_LORA + KV_LORA + LANES])
    kr_even, kr_odd = kr.astype(BF16), pltpu.roll(kr, 64, 1).astype(BF16)
    for h in range(MLA_HEADS):
        base = h * MLA_QK
        q_ref[:, base:base + NOPE_DIM] = q[:, h * NOPE_DIM:(h + 1) * NOPE_DIM].astype(BF16)
        q_ref[:, base + NOPE_DIM:base + MLA_QK] = q_rope[h // 2]
        k_ref[:, base:base + NOPE_DIM] = kv[:, base:base + NOPE_DIM].astype(BF16)
        k_ref[:, base + NOPE_DIM:base + MLA_QK] = kr_even if h % 2 == 0 else kr_odd
        v_ref[:, h * V_DIM:(h + 1) * V_DIM] = kv[:, base + NOPE_DIM:base + MLA_QK].astype(BF16)

    dq = project(1)
    for c in range(dq.shape[1] // LANES):
        sl = slice(c * LANES, (c + 1) * LANES)
        dq_ref[:, sl] = (rope128(dq[:, sl]) * diff_scale).astype(BF16)
    dk = project(2)
    for c in range(dk.shape[1] // LANES):
        sl = slice(c * LANES, (c + 1) * LANES)
        dk_ref[:, sl] = rope128(dk[:, sl]).astype(BF16)
    dv_ref[...] = project(3).astype(BF16)


def _in_proj(x2d, g_mix, w_in_b, g_q, w_uq_b, g_kv, w_ukv_b, tables, tm):
    n, d = x2d.shape
    diff_w = 2 * DIFF_HEADS * DIFF_DIM
    row = lambda i: (i, 0)
    const = lambda i: (0, 0)
    resident = lambda shape: pl.BlockSpec(shape, const, pipeline_mode=pl.Buffered(1))
    kern = functools.partial(
        _in_proj_kernel,
        mla_scale=(NOPE_DIM + ROPE_DIM) ** -0.5 * LOG2E,
        diff_scale=DIFF_DIM ** -0.5 * LOG2E)
    outs = [
        jax.ShapeDtypeStruct((n, MLA_HEADS * MLA_QK), BF16),
        jax.ShapeDtypeStruct((n, MLA_HEADS * MLA_QK), BF16),
        jax.ShapeDtypeStruct((n, MLA_HEADS * V_DIM), BF16),
        jax.ShapeDtypeStruct((n, diff_w), BF16),
        jax.ShapeDtypeStruct((n, diff_w), BF16),
        jax.ShapeDtypeStruct((n, diff_w), BF16),
    ]
    return pl.pallas_call(
        kern,
        grid=(n // tm,),
        in_specs=[
            pl.BlockSpec((tm, d), row),
            pl.BlockSpec((1, d), const),
            resident(w_in_b.shape),
            pl.BlockSpec((1, Q_LORA), const),
            resident(w_uq_b.shape),
            pl.BlockSpec((1, KV_LORA), const),
            resident(w_ukv_b.shape),
        ] + [pl.BlockSpec((tm, LANES), row)] * 5,
        out_specs=[pl.BlockSpec((tm, o.shape[1]), row) for o in outs],
        out_shape=outs,
        scratch_shapes=[pltpu.VMEM((tm, d), BF16)],
        compiler_params=pltpu.CompilerParams(
            dimension_semantics=("parallel",), vmem_limit_bytes=VMEM_LIMIT),
        name="in_proj",
    )(x2d, g_mix, w_in_b, g_q, w_uq_b, g_kv, w_ukv_b, *tables)


def _mla_attn_kernel(q_ref, k_ref, v_ref, o_ref, s0_ref, s1_ref, p0_ref, p1_ref, *, tq):
    nq = q_ref.shape[0] // tq
    s_bufs, p_bufs, inv_l = (s0_ref, s1_ref), (p0_ref, p1_ref), {}
    rows = lambda t: slice(t * tq, (t + 1) * tq)

    def scores(t):
        s_bufs[t % 2][...] = _nt_dot(q_ref[rows(t), :], k_ref[...])

    def softmax(t):
        e, l = _softmax_parts(s_bufs[t % 2][...])
        p_bufs[t % 2][...] = e.astype(BF16)
        inv_l[t] = 1.0 / l

    def pv(t):
        o = jnp.dot(p_bufs[t % 2][...], v_ref[...], preferred_element_type=F32)
        o_ref[rows(t), :] = (o * inv_l.pop(t)).astype(o_ref.dtype)

    scores(0)
    for t in range(nq):
        if t + 1 < nq:
            scores(t + 1)
        softmax(t)
        if t >= 1:
            pv(t - 1)
    pv(nq - 1)


def _mla_attn(q, k, v, batch, seq, tq):
    blk = lambda w: pl.BlockSpec((seq, w), lambda b, h: (b, h))
    return pl.pallas_call(
        functools.partial(_mla_attn_kernel, tq=tq),
        grid=(batch, MLA_HEADS),
        in_specs=[blk(MLA_QK), blk(MLA_QK), blk(V_DIM)],
        out_specs=blk(V_DIM),
        out_shape=jax.ShapeDtypeStruct(v.shape, BF16),
        scratch_shapes=[pltpu.VMEM((tq, seq), F32)] * 2 + [pltpu.VMEM((tq, seq), BF16)] * 2,
        compiler_params=pltpu.CompilerParams(
            dimension_semantics=("parallel", "parallel"), vmem_limit_bytes=VMEM_LIMIT),
        name="mla_attn",
    )(q, k, v)


def _diff_attn_kernel(lq1_ref, lk1_ref, lq2_ref, lk2_ref, q_ref, k_ref, v_ref, g_ref, o_ref,
                      sa0_ref, sa1_ref, sb0_ref, sb1_ref, p0_ref, p1_ref, *, tq, lambda_init):
    lam = (jnp.exp(jnp.sum(lq1_ref[...] * lk1_ref[...], axis=-1, keepdims=True))
           - jnp.exp(jnp.sum(lq2_ref[...] * lk2_ref[...], axis=-1, keepdims=True))
           + lambda_init)
    nq = q_ref.shape[0] // tq
    sa_bufs, sb_bufs, p_bufs, inv_l1 = (sa0_ref, sa1_ref), (sb0_ref, sb1_ref), (p0_ref, p1_ref), {}
    rows = lambda t: slice(t * tq, (t + 1) * tq)

    def scores(t):
        sa_bufs[t % 2][...] = _nt_dot(q_ref[rows(t), :DIFF_DIM], k_ref[:, :DIFF_DIM])
        sb_bufs[t % 2][...] = _nt_dot(q_ref[rows(t), DIFF_DIM:], k_ref[:, DIFF_DIM:])

    def softmax(t):
        e1, l1 = _softmax_parts(sa_bufs[t % 2][...])
        e2, l2 = _softmax_parts(sb_bufs[t % 2][...])
        p_bufs[t % 2][...] = (e1 - e2 * (lam * l1 / l2)).astype(BF16)
        inv_l1[t] = 1.0 / l1

    def pv(t):
        o = jnp.dot(p_bufs[t % 2][...], v_ref[...], preferred_element_type=F32) * inv_l1.pop(t)
        o = _rms(o, g_ref[...], DIFF_NORM_EPS) * (1.0 - lambda_init)
        o_ref[rows(t), :] = o.astype(o_ref.dtype)

    scores(0)
    for t in range(nq):
        if t + 1 < nq:
            scores(t + 1)
        softmax(t)
        if t >= 1:
            pv(t - 1)
    pv(nq - 1)


def _diff_attn(lams, dq, dk, dv, g_sub, batch, seq, tq, lambda_init):
    w = 2 * DIFF_DIM
    blk = pl.BlockSpec((seq, w), lambda b, h: (b, h))
    vec = lambda width: pl.BlockSpec((1, width), lambda b, h: (0, 0))
    return pl.pallas_call(
        functools.partial(_diff_attn_kernel, tq=tq, lambda_init=lambda_init),
        grid=(batch, DIFF_HEADS),
        in_specs=[vec(DIFF_DIM)] * 4 + [blk, blk, blk, vec(w)],
        out_specs=blk,
        out_shape=jax.ShapeDtypeStruct(dv.shape, BF16),
        scratch_shapes=[pltpu.VMEM((tq, seq), F32)] * 4 + [pltpu.VMEM((tq, seq), BF16)] * 2,
        compiler_params=pltpu.CompilerParams(
            dimension_semantics=("parallel", "parallel"), vmem_limit_bytes=VMEM_LIMIT),
        name="diff_attn",
    )(*lams, dq, dk, dv, g_sub)


def _mem_kv_kernel(mem_ref, g_ref, wk_ref, wv_ref, k_ref, v_ref):
    hm = _rms(mem_ref[...], g_ref[...], NORM_EPS).astype(BF16)
    k_ref[...] = jnp.dot(hm, wk_ref[...], preferred_element_type=F32).astype(BF16)
    v_ref[...] = jnp.dot(hm, wv_ref[...], preferred_element_type=F32).astype(BF16)


def _mem_kv(mem2d, g_mem, w_xk_b, w_xv_b, m_len):
    n, d = mem2d.shape
    xw = w_xk_b.shape[1]
    const = lambda b: (0, 0)
    out = jax.ShapeDtypeStruct((n, xw), BF16)
    return pl.pallas_call(
        _mem_kv_kernel,
        grid=(n // m_len,),
        in_specs=[pl.BlockSpec((m_len, d), lambda b: (b, 0)), pl.BlockSpec((1, d), const),
                  pl.BlockSpec((d, xw), const), pl.BlockSpec((d, xw), const)],
        out_specs=[pl.BlockSpec((m_len, xw), lambda b: (b, 0))] * 2,
        out_shape=[out, out],
        compiler_params=pltpu.CompilerParams(
            dimension_semantics=("parallel",), vmem_limit_bytes=VMEM_LIMIT),
        name="mem_kv",
    )(mem2d, g_mem, w_xk_b, w_xv_b)


def _out_xattn_kernel(x_ref, a_ref, b_ref, wo_ref, gx_ref, wxq_ref, xk_ref, xv_ref, wxo_ref, o_ref,
                      *, x_scale):
    half = a_ref.shape[1]
    x1 = (x_ref[...]
          + jnp.dot(a_ref[...], wo_ref[:half, :], preferred_element_type=F32)
          + jnp.dot(b_ref[...], wo_ref[half:, :], preferred_element_type=F32))
    hx = _rms(x1, gx_ref[...], NORM_EPS).astype(BF16)
    xq = (jnp.dot(hx, wxq_ref[...], preferred_element_type=F32) * x_scale).astype(BF16)
    heads = []
    for h in range(X_HEADS):
        sl = slice(h * X_DIM, (h + 1) * X_DIM)
        e, l = _softmax_parts(_nt_dot(xq[:, sl], xk_ref[:, sl]))
        o = jnp.dot(e.astype(BF16), xv_ref[:, sl], preferred_element_type=F32)
        heads.append((o * (1.0 / l)).astype(BF16))
    xo = jnp.concatenate(heads, axis=-1)
    o_ref[...] = x1 + jnp.dot(xo, wxo_ref[...], preferred_element_type=F32)


def _out_xattn(x2d, a, b, w_out_b, g_x, w_xq_b, xk, xv, w_xo_b, seq, m_len, tm):
    n, d = x2d.shape
    xw = w_xq_b.shape[1]
    per_b = seq // tm
    row = lambda i: (i, 0)
    const = lambda i: (0, 0)
    mem = lambda i: (i // per_b, 0)
    return pl.pallas_call(
        functools.partial(_out_xattn_kernel, x_scale=X_DIM ** -0.5 * LOG2E),
        grid=(n // tm,),
        in_specs=[
            pl.BlockSpec((tm, d), row),
            pl.BlockSpec((tm, a.shape[1]), row),
            pl.BlockSpec((tm, b.shape[1]), row),
            pl.BlockSpec(w_out_b.shape, const),
            pl.BlockSpec((1, d), const),
            pl.BlockSpec(w_xq_b.shape, const),
            pl.BlockSpec((m_len, xw), mem),
            pl.BlockSpec((m_len, xw), mem),
            pl.BlockSpec(w_xo_b.shape, const),
        ],
        out_specs=pl.BlockSpec((tm, d), row),
        out_shape=jax.ShapeDtypeStruct((n, d), F32),
        compiler_params=pltpu.CompilerParams(
            dimension_semantics=("parallel",), vmem_limit_bytes=VMEM_LIMIT),
        name="out_xattn",
    )(x2d, a, b, w_out_b, g_x, w_xq_b, xk, xv, w_xo_b)


def _ffn_kernel(x_ref, g_ref, wg_ref, wu_ref, wd_ref, gf_ref, o_ref, h_ref, *, final_norm, row_split):
    j = pl.program_id(1)
    nj = pl.num_programs(1)
    tr = x_ref.shape[0] // row_split

    def step(first, last):
        for r in range(row_split):
            rows = slice(r * tr, (r + 1) * tr)
            if first:
                h_ref[rows, :] = _rms(x_ref[rows, :], g_ref[...], NORM_EPS).astype(BF16)
            h = h_ref[rows, :]
            gate = jnp.dot(h, wg_ref[...], preferred_element_type=F32)
            up = jnp.dot(h, wu_ref[...], preferred_element_type=F32)
            act = (gate * jax.nn.sigmoid(gate) * up).astype(BF16)
            y = (x_ref[rows, :] if first else o_ref[rows, :]) + jnp.dot(
                act, wd_ref[...], preferred_element_type=F32)
            if last and final_norm:
                y = _rms(y, gf_ref[...], NORM_EPS)
            o_ref[rows, :] = y

    pl.when(j == 0)(lambda: step(True, False))
    pl.when((j > 0) & (j < nj - 1))(lambda: step(False, False))
    pl.when(j == nj - 1)(lambda: step(False, True))


def _ffn(x2d, g_ffn, w_gate_b, w_up_b, w_down_b, g_final, tm, th, final_norm):
    n, d = x2d.shape
    hidden = w_gate_b.shape[1]
    assert hidden // th >= 2
    row = lambda i, j: (i, 0)
    const = lambda i, j: (0, 0)
    return pl.pallas_call(
        functools.partial(_ffn_kernel, final_norm=final_norm, row_split=2),
        grid=(n // tm, hidden // th),
        in_specs=[
            pl.BlockSpec((tm, d), row),
            pl.BlockSpec((1, d), const),
            pl.BlockSpec((d, th), lambda i, j: (0, j)),
            pl.BlockSpec((d, th), lambda i, j: (0, j)),
            pl.BlockSpec((th, d), lambda i, j: (j, 0)),
            pl.BlockSpec((1, d), const),
        ],
        out_specs=pl.BlockSpec((tm, d), row),
        out_shape=jax.ShapeDtypeStruct((n, d), F32),
        scratch_shapes=[pltpu.VMEM((tm, d), BF16)],
        compiler_params=pltpu.CompilerParams(
            dimension_semantics=("parallel", "arbitrary"), vmem_limit_bytes=VMEM_LIMIT),
        name="ffn",
    )(x2d, g_ffn, w_gate_b, w_up_b, w_down_b, g_final)


def kernel(x, mem, positions, g_mix, w_in, g_q_lat, w_uq, g_kv_lat, w_ukv, lambda_q1, lambda_k1,
           lambda_q2, lambda_k2, g_diff_sub, w_out, g_xattn, g_mem, w_xq, w_xk, w_xv, w_xo,
           g_ffn, w_gate, w_up, w_down, g_final):
    batch, seq, d = x.shape
    m_len = mem.shape[1]
    depth = w_in.shape[0]
    n = batch * seq
    lat_w = Q_LORA + KV_LORA + ROPE_DIM

    tables = _rope_tables(positions, tile=2048)

    head_w = NOPE_DIM + ROPE_DIM
    uq_cols = np.concatenate(
        [h * head_w + np.arange(NOPE_DIM) for h in range(MLA_HEADS)]
        + [h * head_w + NOPE_DIM + np.arange(ROPE_DIM) for h in range(MLA_HEADS)])

    x2d = x.reshape(n, d)
    mem2d = mem.reshape(batch * m_len, d)
    row = lambda v: v.reshape(1, -1)

    for layer in range(depth):
        lambda_init = 0.8 - 0.6 * math.exp(-0.3 * layer)
        w_in_l = w_in[layer]
        w_in_b = jnp.zeros((d, LAT_PAD + w_in_l.shape[1] - lat_w), BF16)
        w_in_b = lax.dynamic_update_slice(w_in_b, w_in_l[:, :lat_w].astype(BF16), (0, 0))
        w_in_b = lax.dynamic_update_slice(w_in_b, w_in_l[:, lat_w:].astype(BF16), (0, LAT_PAD))
        q, k, v, dq, dk, dv = _in_proj(
            x2d, row(g_mix[layer]), w_in_b, row(g_q_lat[layer]),
            w_uq[layer][:, uq_cols].astype(BF16), row(g_kv_lat[layer]), w_ukv[layer].astype(BF16),
            tables, tm=512)
        out_mla = _mla_attn(q, k, v, batch, seq, tq=512)
        out_diff = _diff_attn(
            [row(lambda_q1[layer]), row(lambda_k1[layer]), row(lambda_q2[layer]), row(lambda_k2[layer])],
            dq, dk, dv, row(g_diff_sub[layer]), batch, seq, tq=512, lambda_init=lambda_init)
        xk, xv = _mem_kv(mem2d, row(g_mem[layer]), w_xk[layer].astype(BF16),
                         w_xv[layer].astype(BF16), m_len)
        x2d = _out_xattn(x2d, out_mla, out_diff, w_out[layer].astype(BF16), row(g_xattn[layer]),
                         w_xq[layer].astype(BF16), xk, xv, w_xo[layer].astype(BF16),
                         seq, m_len, tm=512)
        x2d = _ffn(x2d, row(g_ffn[layer]), w_gate[layer].astype(BF16), w_up[layer].astype(BF16),
                   w_down[layer].astype(BF16), row(g_final), tm=1024, th=512,
                   final_norm=(layer == depth - 1))
    return x2d.reshape(batch, seq, d)
```

```python
import functools
import math

import numpy as np
import jax
import jax.numpy as jnp
from jax import lax
from jax.experimental import pallas as pl
from jax.experimental.pallas import tpu as pltpu

F32 = jnp.float32
BF16 = jnp.bfloat16

ROPE_THETA = 10000.0
NORM_EPS = 1e-6
DIFF_NORM_EPS = 1e-5
Q_LORA, KV_LORA, ROPE_DIM = 512, 256, 64
MLA_HEADS, NOPE_DIM, V_DIM = 8, 128, 128
DIFF_HEADS, DIFF_DIM = 4, 128
X_HEADS, X_DIM = 4, 128
LANES = 128
LOG2E = math.log2(math.e)
VMEM_LIMIT = 58 * 1024 * 1024

LAT_W = Q_LORA + KV_LORA + ROPE_DIM
LAT_PAD = 1024
MLA_QK = 2 * LANES
VT_ROWS = V_DIM + 16


def _rms(x, g, eps):
    return x * lax.rsqrt(jnp.mean(x * x, axis=-1, keepdims=True) + eps) * g


def _nt_dot(a, b):
    return lax.dot_general(a, b, (((1,), (1,)), ((), ())), preferred_element_type=F32)


def _softmax_parts(s):
    m = jnp.max(s, axis=-1, keepdims=True)
    e = jnp.exp2(s - m)
    return e, jnp.sum(e, axis=-1, keepdims=True)


def _rope_table_kernel(pos_ref, freq_ref, cosd_ref, sind_ref, cosm_ref, sinlo_ref, sinhi_ref):
    ang = pos_ref[...].astype(F32) * freq_ref[...]
    ct, st = jnp.cos(ang), jnp.sin(ang)
    ct_r, st_r = pltpu.roll(ct, 64, 1), pltpu.roll(st, 64, 1)
    lane = lax.broadcasted_iota(jnp.int32, ang.shape, 1)
    lo = lane < 64
    cosd_ref[...] = jnp.where(lo, ct, ct_r)
    sind_ref[...] = jnp.where(lo, -st, st_r)
    cosm_ref[...] = jnp.where(lo, ct_r, ct)
    sm = jnp.where(lo, st_r, st)
    first_half = (lane & 63) < 32
    sinlo_ref[...] = jnp.where(first_half, -sm, 0.0)
    sinhi_ref[...] = jnp.where(first_half, 0.0, sm)


def _rope_table_and_w_in_kernel(pos_ref, freq_ref, win_ref, *out_refs):
    *table_refs, wout_ref = out_refs
    _rope_table_kernel(pos_ref, freq_ref, *table_refs)
    w = win_ref[...]
    wout_ref[:, :LAT_W] = w[:, :LAT_W].astype(BF16)
    wout_ref[:, LAT_W:LAT_PAD] = jnp.zeros((w.shape[0], LAT_PAD - LAT_W), BF16)
    wout_ref[:, LAT_PAD:] = w[:, LAT_W:].astype(BF16)


def _rope_tables(positions, w_in_l, tile):
    n = positions.size
    steps = n // tile
    d, in_w = w_in_l.shape
    slab = d // steps
    assert slab * steps == d and slab % 16 == 0
    inv_d = ROPE_THETA ** (-jnp.arange(0, DIFF_DIM, 2, dtype=F32) / DIFF_DIM)
    inv_m = ROPE_THETA ** (-jnp.arange(0, ROPE_DIM, 2, dtype=F32) / ROPE_DIM)
    freq = jnp.concatenate([inv_d, inv_m, inv_m]).reshape(1, LANES)
    tab = jax.ShapeDtypeStruct((n, LANES), F32)
    tspec = pl.BlockSpec((tile, LANES), lambda i: (i, 0))
    packed_w = in_w - LAT_W + LAT_PAD
    *tables, w_in_b = pl.pallas_call(
        _rope_table_and_w_in_kernel,
        grid=(steps,),
        in_specs=[pl.BlockSpec((tile, 1), lambda i: (i, 0)), pl.BlockSpec((1, LANES), lambda i: (0, 0)),
                  pl.BlockSpec((slab, in_w), lambda i: (i, 0))],
        out_specs=[tspec] * 5 + [pl.BlockSpec((slab, packed_w), lambda i: (i, 0))],
        out_shape=[tab] * 5 + [jax.ShapeDtypeStruct((d, packed_w), BF16)],
        compiler_params=pltpu.CompilerParams(
            dimension_semantics=("parallel",), vmem_limit_bytes=VMEM_LIMIT),
        name="rope_tables",
    )(positions.reshape(n, 1), freq, w_in_l)
    return tables, w_in_b


def _in_proj_kernel(x_ref, gmix_ref, w_ref, gq_ref, wuq_ref, gkv_ref, wukv_ref,
                    cosd_ref, sind_ref, cosm_ref, sinlo_ref, sinhi_ref,
                    q_ref, k_ref, vt_ref, dq_ref, dk_ref, dv_ref, h_ref,
                    *, mla_scale, diff_scale):
    def project(group):
        cols = slice(group * LAT_PAD, (group + 1) * LAT_PAD)
        return jnp.dot(h_ref[...], w_ref[:, cols], preferred_element_type=F32)

    def rope64(t):
        return (t * cosm_ref[...] + pltpu.roll(t, 96, 1) * sinlo_ref[...]
                + pltpu.roll(t, 32, 1) * sinhi_ref[...])

    def rope128(t):
        return t * cosd_ref[...] + pltpu.roll(t, 64, 1) * sind_ref[...]

    h_ref[...] = _rms(x_ref[...], gmix_ref[...], NORM_EPS).astype(BF16)

    lat = project(0)
    c_q = _rms(lat[:, :Q_LORA], gq_ref[...], NORM_EPS).astype(BF16)
    q = jnp.dot(c_q, wuq_ref[...], preferred_element_type=F32) * mla_scale
    nope_w = MLA_HEADS * NOPE_DIM
    q_rope = [rope64(q[:, nope_w + c * LANES: nope_w + (c + 1) * LANES]).astype(BF16)
              for c in range(MLA_HEADS // 2)]
    c_kv = _rms(lat[:, Q_LORA:Q_LORA + KV_LORA], gkv_ref[...], NORM_EPS).astype(BF16)
    kv = jnp.dot(c_kv, wukv_ref[...], preferred_element_type=F32)
    kr = rope64(lat[:, Q_LORA + KV_LORA:Q_LORA + KV_LORA + LANES])
    kr_even, kr_odd = kr.astype(BF16), pltpu.roll(kr, 64, 1).astype(BF16)
    for h in range(MLA_HEADS):
        base = h * MLA_QK
        q_ref[:, base:base + NOPE_DIM] = q[:, h * NOPE_DIM:(h + 1) * NOPE_DIM].astype(BF16)
        q_ref[:, base + NOPE_DIM:base + MLA_QK] = q_rope[h // 2]
        k_ref[:, base:base + NOPE_DIM] = kv[:, base:base + NOPE_DIM].astype(BF16)
        k_ref[:, base + NOPE_DIM:base + MLA_QK] = kr_even if h % 2 == 0 else kr_odd
        vt_base = h * VT_ROWS
        vt_ref[0, vt_base:vt_base + V_DIM, :] = kv[:, base + NOPE_DIM:base + MLA_QK].T.astype(BF16)
        pad_row = lax.broadcasted_iota(jnp.int32, (VT_ROWS - V_DIM, kv.shape[0]), 0)
        vt_ref[0, vt_base + V_DIM:vt_base + VT_ROWS, :] = jnp.where(pad_row == 0, 1.0, 0.0).astype(BF16)

    dq = project(1)
    for c in range(dq.shape[1] // LANES):
        sl = slice(c * LANES, (c + 1) * LANES)
        dq_ref[:, sl] = (rope128(dq[:, sl]) * diff_scale).astype(BF16)
    dk = project(2)
    for c in range(dk.shape[1] // LANES):
        sl = slice(c * LANES, (c + 1) * LANES)
        dk_ref[:, sl] = rope128(dk[:, sl]).astype(BF16)
    dv_ref[...] = project(3).astype(BF16)


def _in_proj(x2d, g_mix, w_in_b, g_q, w_uq_b, g_kv, w_ukv_b, tables, seq, tm):
    n, d = x2d.shape
    per_b = seq // tm
    diff_w = 2 * DIFF_HEADS * DIFF_DIM
    row = lambda i: (i, 0)
    const = lambda i: (0, 0)
    resident = lambda shape: pl.BlockSpec(shape, const, pipeline_mode=pl.Buffered(1))
    kern = functools.partial(
        _in_proj_kernel,
        mla_scale=(NOPE_DIM + ROPE_DIM) ** -0.5 * LOG2E,
        diff_scale=DIFF_DIM ** -0.5 * LOG2E)
    outs = [
        jax.ShapeDtypeStruct((n, MLA_HEADS * MLA_QK), BF16),
        jax.ShapeDtypeStruct((n, MLA_HEADS * MLA_QK), BF16),
        jax.ShapeDtypeStruct((n // seq, MLA_HEADS * VT_ROWS, seq), BF16),
        jax.ShapeDtypeStruct((n, diff_w), BF16),
        jax.ShapeDtypeStruct((n, diff_w), BF16),
        jax.ShapeDtypeStruct((n, diff_w), BF16),
    ]
    return pl.pallas_call(
        kern,
        grid=(n // tm,),
        in_specs=[
            pl.BlockSpec((tm, d), row),
            pl.BlockSpec((1, d), const),
            resident(w_in_b.shape),
            pl.BlockSpec((1, Q_LORA), const),
            resident(w_uq_b.shape),
            pl.BlockSpec((1, KV_LORA), const),
            resident(w_ukv_b.shape),
        ] + [pl.BlockSpec((tm, LANES), row)] * 5,
        out_specs=[pl.BlockSpec((tm, o.shape[1]), row) if o.ndim == 2 else
                   pl.BlockSpec((1, o.shape[1], tm), lambda i: (i // per_b, 0, i % per_b)) for o in outs],
        out_shape=outs,
        scratch_shapes=[pltpu.VMEM((tm, d), BF16)],
        compiler_params=pltpu.CompilerParams(
            dimension_semantics=("parallel",), vmem_limit_bytes=VMEM_LIMIT),
        name="in_proj",
    )(x2d, g_mix, w_in_b, g_q, w_uq_b, g_kv, w_ukv_b, *tables)


def _cast_slab_specs(weights, n_steps, step_of):
    in_specs, out_specs, out_shapes = [], [], []
    for w in weights:
        slab = w.shape[0] // n_steps
        assert slab * n_steps == w.shape[0] and slab % 16 == 0, (w.shape, n_steps)
        spec = pl.BlockSpec((slab, w.shape[1]), lambda *ids: (step_of(*ids), 0))
        in_specs.append(spec)
        out_specs.append(spec)
        out_shapes.append(jax.ShapeDtypeStruct(w.shape, BF16))
    return in_specs, out_specs, out_shapes


def _cast_slabs(src_refs, dst_refs):
    for src, dst in zip(src_refs, dst_refs):
        dst[...] = src[...].astype(BF16)


def _mla_attn_kernel(q_ref, k_ref, vt_ref, *refs, tq, n_cast):
    cast_src, (o_ref, *cast_dst) = refs[:n_cast], refs[n_cast:2 * n_cast + 1]
    s0_ref, s1_ref, p0_ref, p1_ref = refs[2 * n_cast + 1:]
    _cast_slabs(cast_src, cast_dst)
    nq = q_ref.shape[0] // tq
    s_bufs, p_bufs = (s0_ref, s1_ref), (p0_ref, p1_ref)
    rows = lambda t: slice(t * tq, (t + 1) * tq)

    def scores(t):
        s_bufs[t % 2][...] = _nt_dot(q_ref[rows(t), :], k_ref[...])

    def softmax(t):
        s = s_bufs[t % 2][...]
        p_bufs[t % 2][...] = jnp.exp2(s - jnp.max(s, axis=-1, keepdims=True)).astype(BF16)

    def pv(t):
        ot = _nt_dot(vt_ref[...], p_bufs[t % 2][...])
        o = ot[:V_DIM, :] * (1.0 / ot[V_DIM:V_DIM + 1, :])
        o_ref[rows(t), :] = o.T.astype(o_ref.dtype)

    scores(0)
    for t in range(nq):
        if t + 1 < nq:
            scores(t + 1)
        softmax(t)
        if t >= 1:
            pv(t - 1)
    pv(nq - 1)


def _mla_attn(q, k, vt, cast_weights, batch, seq, tq):
    blk = lambda w: pl.BlockSpec((seq, w), lambda b, h: (b, h))
    vt = vt.reshape(batch * MLA_HEADS * VT_ROWS, seq)
    cast_in, cast_out, cast_shapes = _cast_slab_specs(
        cast_weights, batch * MLA_HEADS, lambda b, h: b * MLA_HEADS + h)
    return pl.pallas_call(
        functools.partial(_mla_attn_kernel, tq=tq, n_cast=len(cast_weights)),
        grid=(batch, MLA_HEADS),
        in_specs=[blk(MLA_QK), blk(MLA_QK),
                  pl.BlockSpec((VT_ROWS, seq), lambda b, h: (b * MLA_HEADS + h, 0))] + cast_in,
        out_specs=[blk(V_DIM)] + cast_out,
        out_shape=[jax.ShapeDtypeStruct((batch * seq, MLA_HEADS * V_DIM), BF16)] + cast_shapes,
        scratch_shapes=[pltpu.VMEM((tq, seq), F32)] * 2 + [pltpu.VMEM((tq, seq), BF16)] * 2,
        compiler_params=pltpu.CompilerParams(
            dimension_semantics=("parallel", "parallel"), vmem_limit_bytes=VMEM_LIMIT),
        name="mla_attn",
    )(q, k, vt, *cast_weights)


def _diff_attn_kernel(lq1_ref, lk1_ref, lq2_ref, lk2_ref, q_ref, k_ref, v_ref, g_ref, *refs,
                      tq, lambda_init, n_cast):
    cast_src, (o_ref, *cast_dst) = refs[:n_cast], refs[n_cast:2 * n_cast + 1]
    sa0_ref, sa1_ref, sb0_ref, sb1_ref, p0_ref, p1_ref = refs[2 * n_cast + 1:]
    _cast_slabs(cast_src, cast_dst)
    lam = (jnp.exp(jnp.sum(lq1_ref[...] * lk1_ref[...], axis=-1, keepdims=True))
           - jnp.exp(jnp.sum(lq2_ref[...] * lk2_ref[...], axis=-1, keepdims=True))
           + lambda_init)
    nq = q_ref.shape[0] // tq
    sa_bufs, sb_bufs, p_bufs, inv_l1 = (sa0_ref, sa1_ref), (sb0_ref, sb1_ref), (p0_ref, p1_ref), {}
    rows = lambda t: slice(t * tq, (t + 1) * tq)

    def scores(t):
        sa_bufs[t % 2][...] = _nt_dot(q_ref[rows(t), :DIFF_DIM], k_ref[:, :DIFF_DIM])
        sb_bufs[t % 2][...] = _nt_dot(q_ref[rows(t), DIFF_DIM:], k_ref[:, DIFF_DIM:])

    def softmax(t):
        e1, l1 = _softmax_parts(sa_bufs[t % 2][...])
        e2, l2 = _softmax_parts(sb_bufs[t % 2][...])
        p_bufs[t % 2][...] = (e1 - e2 * (lam * l1 / l2)).astype(BF16)
        inv_l1[t] = 1.0 / l1

    def pv(t):
        o = jnp.dot(p_bufs[t % 2][...], v_ref[...], preferred_element_type=F32) * inv_l1.pop(t)
        o = _rms(o, g_ref[...], DIFF_NORM_EPS) * (1.0 - lambda_init)
        o_ref[rows(t), :] = o.astype(o_ref.dtype)

    scores(0)
    for t in range(nq):
        if t + 1 < nq:
            scores(t + 1)
        softmax(t)
        if t >= 1:
            pv(t - 1)
    pv(nq - 1)


def _diff_attn(lams, dq, dk, dv, g_sub, cast_weights, batch, seq, tq, lambda_init):
    w = 2 * DIFF_DIM
    blk = pl.BlockSpec((seq, w), lambda b, h: (b, h))
    vec = lambda width: pl.BlockSpec((1, width), lambda b, h: (0, 0))
    cast_in, cast_out, cast_shapes = _cast_slab_specs(
        cast_weights, batch * DIFF_HEADS, lambda b, h: b * DIFF_HEADS + h)
    return pl.pallas_call(
        functools.partial(_diff_attn_kernel, tq=tq, lambda_init=lambda_init, n_cast=len(cast_weights)),
        grid=(batch, DIFF_HEADS),
        in_specs=[vec(DIFF_DIM)] * 4 + [blk, blk, blk, vec(w)] + cast_in,
        out_specs=[blk] + cast_out,
        out_shape=[jax.ShapeDtypeStruct(dv.shape, BF16)] + cast_shapes,
        scratch_shapes=[pltpu.VMEM((tq, seq), F32)] * 4 + [pltpu.VMEM((tq, seq), BF16)] * 2,
        compiler_params=pltpu.CompilerParams(
            dimension_semantics=("parallel", "parallel"), vmem_limit_bytes=VMEM_LIMIT),
        name="diff_attn",
    )(*lams, dq, dk, dv, g_sub, *cast_weights)


def _mem_kv_kernel(mem_ref, g_ref, wk_ref, wv_ref, k_ref, v_ref):
    hm = _rms(mem_ref[...], g_ref[...], NORM_EPS).astype(BF16)
    k_ref[...] = jnp.dot(hm, wk_ref[...], preferred_element_type=F32).astype(BF16)
    v_ref[...] = jnp.dot(hm, wv_ref[...], preferred_element_type=F32).astype(BF16)


def _mem_kv(mem2d, g_mem, w_xk_b, w_xv_b, m_len):
    n, d = mem2d.shape
    xw = w_xk_b.shape[1]
    const = lambda b: (0, 0)
    out = jax.ShapeDtypeStruct((n, xw), BF16)
    return pl.pallas_call(
        _mem_kv_kernel,
        grid=(n // m_len,),
        in_specs=[pl.BlockSpec((m_len, d), lambda b: (b, 0)), pl.BlockSpec((1, d), const),
                  pl.BlockSpec((d, xw), const), pl.BlockSpec((d, xw), const)],
        out_specs=[pl.BlockSpec((m_len, xw), lambda b: (b, 0))] * 2,
        out_shape=[out, out],
        compiler_params=pltpu.CompilerParams(
            dimension_semantics=("parallel",), vmem_limit_bytes=VMEM_LIMIT),
        name="mem_kv",
    )(mem2d, g_mem, w_xk_b, w_xv_b)


def _out_xattn_kernel(x_ref, a_ref, b_ref, wo_ref, gx_ref, wxq_ref, xk_ref, xv_ref, wxo_ref, o_ref,
                      *, x_scale):
    half = a_ref.shape[1]
    x1 = (x_ref[...]
          + jnp.dot(a_ref[...], wo_ref[:half, :], preferred_element_type=F32)
          + jnp.dot(b_ref[...], wo_ref[half:, :], preferred_element_type=F32))
    hx = _rms(x1, gx_ref[...], NORM_EPS).astype(BF16)
    xq = (jnp.dot(hx, wxq_ref[...], preferred_element_type=F32) * x_scale).astype(BF16)
    heads = []
    for h in range(X_HEADS):
        sl = slice(h * X_DIM, (h + 1) * X_DIM)
        e, l = _softmax_parts(_nt_dot(xq[:, sl], xk_ref[:, sl]))
        o = jnp.dot(e.astype(BF16), xv_ref[:, sl], preferred_element_type=F32)
        heads.append((o * (1.0 / l)).astype(BF16))
    xo = jnp.concatenate(heads, axis=-1)
    o_ref[...] = x1 + jnp.dot(xo, wxo_ref[...], preferred_element_type=F32)


def _out_xattn(x2d, a, b, w_out_b, g_x, w_xq_b, xk, xv, w_xo_b, seq, m_len, tm):
    n, d = x2d.shape
    xw = w_xq_b.shape[1]
    per_b = seq // tm
    row = lambda i: (i, 0)
    const = lambda i: (0, 0)
    mem = lambda i: (i // per_b, 0)
    return pl.pallas_call(
        functools.partial(_out_xattn_kernel, x_scale=X_DIM ** -0.5 * LOG2E),
        grid=(n // tm,),
        in_specs=[
            pl.BlockSpec((tm, d), row),
            pl.BlockSpec((tm, a.shape[1]), row),
            pl.BlockSpec((tm, b.shape[1]), row),
            pl.BlockSpec(w_out_b.shape, const),
            pl.BlockSpec((1, d), const),
            pl.BlockSpec(w_xq_b.shape, const),
            pl.BlockSpec((m_len, xw), mem),
            pl.BlockSpec((m_len, xw), mem),
            pl.BlockSpec(w_xo_b.shape, const),
        ],
        out_specs=pl.BlockSpec((tm, d), row),
        out_shape=jax.ShapeDtypeStruct((n, d), F32),
        compiler_params=pltpu.CompilerParams(
            dimension_semantics=("parallel",), vmem_limit_bytes=VMEM_LIMIT),
        name="out_xattn",
    )(x2d, a, b, w_out_b, g_x, w_xq_b, xk, xv, w_xo_b)


def _ffn_kernel(x_ref, g_ref, wg_ref, wu_ref, wd_ref, gf_ref, o_ref, h_ref, *, final_norm, row_split):
    j = pl.program_id(1)
    nj = pl.num_programs(1)
    tr = x_ref.shape[0] // row_split

    def step(first, last):
        for r in range(row_split):
            rows = slice(r * tr, (r + 1) * tr)
            if first:
                h_ref[rows, :] = _rms(x_ref[rows, :], g_ref[...], NORM_EPS).astype(BF16)
            h = h_ref[rows, :]
            gate = jnp.dot(h, wg_ref[...], preferred_element_type=F32)
            up = jnp.dot(h, wu_ref[...], preferred_element_type=F32)
            act = (gate * jax.nn.sigmoid(gate) * up).astype(BF16)
            y = (x_ref[rows, :] if first else o_ref[rows, :]) + jnp.dot(
                act, wd_ref[...], preferred_element_type=F32)
            if last and final_norm:
                y = _rms(y, gf_ref[...], NORM_EPS)
            o_ref[rows, :] = y

    pl.when(j == 0)(lambda: step(True, False))
    pl.when((j > 0) & (j < nj - 1))(lambda: step(False, False))
    pl.when(j == nj - 1)(lambda: step(False, True))


def _ffn(x2d, g_ffn, w_gate_b, w_up_b, w_down_b, g_final, tm, th, final_norm):
    n, d = x2d.shape
    hidden = w_gate_b.shape[1]
    assert hidden // th >= 2
    row = lambda i, j: (i, 0)
    const = lambda i, j: (0, 0)
    return pl.pallas_call(
        functools.partial(_ffn_kernel, final_norm=final_norm, row_split=2),
        grid=(n // tm, hidden // th),
        in_specs=[
            pl.BlockSpec((tm, d), row),
            pl.BlockSpec((1, d), const),
            pl.BlockSpec((d, th), lambda i, j: (0, j)),
            pl.BlockSpec((d, th), lambda i, j: (0, j)),
            pl.BlockSpec((th, d), lambda i, j: (j, 0)),
            pl.BlockSpec((1, d), const),
        ],
        out_specs=pl.BlockSpec((tm, d), row),
        out_shape=jax.ShapeDtypeStruct((n, d), F32),
        scratch_shapes=[pltpu.VMEM((tm, d), BF16)],
        compiler_params=pltpu.CompilerParams(
            dimension_semantics=("parallel", "arbitrary"), vmem_limit_bytes=VMEM_LIMIT),
        name="ffn",
    )(x2d, g_ffn, w_gate_b, w_up_b, w_down_b, g_final)


def kernel(x, mem, positions, g_mix, w_in, g_q_lat, w_uq, g_kv_lat, w_ukv, lambda_q1, lambda_k1,
           lambda_q2, lambda_k2, g_diff_sub, w_out, g_xattn, g_mem, w_xq, w_xk, w_xv, w_xo,
           g_ffn, w_gate, w_up, w_down, g_final):
    batch, seq, d = x.shape
    m_len = mem.shape[1]
    depth = w_in.shape[0]
    n = batch * seq

    head_w = NOPE_DIM + ROPE_DIM
    uq_cols = np.concatenate(
        [h * head_w + np.arange(NOPE_DIM) for h in range(MLA_HEADS)]
        + [h * head_w + NOPE_DIM + np.arange(ROPE_DIM) for h in range(MLA_HEADS)])

    x2d = x.reshape(n, d)
    mem2d = mem.reshape(batch * m_len, d)
    row = lambda v: v.reshape(1, -1)

    for layer in range(depth):
        lambda_init = 0.8 - 0.6 * math.exp(-0.3 * layer)
        tables, w_in_b = _rope_tables(positions, w_in[layer], tile=2048)
        q, k, v, dq, dk, dv = _in_proj(
            x2d, row(g_mix[layer]), w_in_b, row(g_q_lat[layer]),
            w_uq[layer][:, uq_cols].astype(BF16), row(g_kv_lat[layer]), w_ukv[layer].astype(BF16),
            tables, seq, tm=512)
        out_mla, w_gate_b, w_up_b = _mla_attn(
            q, k, v, [w_gate[layer], w_up[layer]], batch, seq, tq=512)
        out_diff, w_down_b, w_out_b, w_xq_b, w_xk_b, w_xv_b, w_xo_b = _diff_attn(
            [row(lambda_q1[layer]), row(lambda_k1[layer]), row(lambda_q2[layer]), row(lambda_k2[layer])],
            dq, dk, dv, row(g_diff_sub[layer]),
            [w_down[layer], w_out[layer], w_xq[layer], w_xk[layer], w_xv[layer], w_xo[layer]],
            batch, seq, tq=512, lambda_init=lambda_init)
        xk, xv = _mem_kv(mem2d, row(g_mem[layer]), w_xk_b, w_xv_b, m_len)
        x2d = _out_xattn(x2d, out_mla, out_diff, w_out_b, row(g_xattn[layer]),
                         w_xq_b, xk, xv, w_xo_b, seq, m_len, tm=512)
        x2d = _ffn(x2d, row(g_ffn[layer]), w_gate_b, w_up_b, w_down_b, row(g_final),
                   tm=1024, th=512, final_norm=(layer == depth - 1))
    return x2d.reshape(batch, seq, d)
```

```python
import functools
import math

import numpy as np
import jax
import jax.numpy as jnp
from jax import lax
from jax.experimental import pallas as pl
from jax.experimental.pallas import tpu as pltpu

F32 = jnp.float32
BF16 = jnp.bfloat16

ROPE_THETA = 10000.0
NORM_EPS = 1e-6
DIFF_NORM_EPS = 1e-5
Q_LORA, KV_LORA, ROPE_DIM = 512, 256, 64
MLA_HEADS, NOPE_DIM, V_DIM = 8, 128, 128
DIFF_HEADS, DIFF_DIM = 4, 128
X_HEADS, X_DIM = 4, 128
LANES = 128
LOG2E = math.log2(math.e)
VMEM_LIMIT = 58 * 1024 * 1024

LAT_W = Q_LORA + KV_LORA + ROPE_DIM
LAT_DOT_W = Q_LORA + KV_LORA + LANES
DIFF_W = 2 * DIFF_HEADS * DIFF_DIM
MLA_QK = 2 * LANES
V_EXT = 2 * LANES


def _rms(x, g, eps):
    return x * lax.rsqrt(jnp.mean(x * x, axis=-1, keepdims=True) + eps) * g


def _nt_dot(a, b):
    return lax.dot_general(a, b, (((1,), (1,)), ((), ())), preferred_element_type=F32)


def _softmax_parts(s):
    m = jnp.max(s, axis=-1, keepdims=True)
    e = jnp.exp2(s - m)
    return e, jnp.sum(e, axis=-1, keepdims=True)


def _rope_table_kernel(pos_ref, freq_ref, cosd_ref, sind_ref, cosm_ref, sinlo_ref, sinhi_ref):
    ang = pos_ref[...].astype(F32) * freq_ref[...]
    ct, st = jnp.cos(ang), jnp.sin(ang)
    ct_r, st_r = pltpu.roll(ct, 64, 1), pltpu.roll(st, 64, 1)
    lane = lax.broadcasted_iota(jnp.int32, ang.shape, 1)
    lo = lane < 64
    cosd_ref[...] = jnp.where(lo, ct, ct_r)
    sind_ref[...] = jnp.where(lo, -st, st_r)
    cosm_ref[...] = jnp.where(lo, ct_r, ct)
    sm = jnp.where(lo, st_r, st)
    first_half = (lane & 63) < 32
    sinlo_ref[...] = jnp.where(first_half, -sm, 0.0)
    sinhi_ref[...] = jnp.where(first_half, 0.0, sm)


def _rope_table_and_w_in_kernel(pos_ref, freq_ref, wt_ref, *out_refs):
    *table_refs, wt_out_ref = out_refs
    _rope_table_kernel(pos_ref, freq_ref, *table_refs)
    wt_out_ref[...] = wt_ref[...].astype(BF16)


def _rope_tables(positions, w_in_t, tile):
    n = positions.size
    steps = n // tile
    rows, d = w_in_t.shape
    n_slabs = max(s for s in range(1, steps + 1) if steps % s == 0 and rows % (16 * s) == 0)
    slab, revisit = rows // n_slabs, steps // n_slabs
    inv_d = ROPE_THETA ** (-jnp.arange(0, DIFF_DIM, 2, dtype=F32) / DIFF_DIM)
    inv_m = ROPE_THETA ** (-jnp.arange(0, ROPE_DIM, 2, dtype=F32) / ROPE_DIM)
    freq = jnp.concatenate([inv_d, inv_m, inv_m]).reshape(1, LANES)
    tab = jax.ShapeDtypeStruct((n, LANES), F32)
    tspec = pl.BlockSpec((tile, LANES), lambda i: (i, 0))
    wspec = pl.BlockSpec((slab, d), lambda i: (i // revisit, 0))
    *tables, w_in_tb = pl.pallas_call(
        _rope_table_and_w_in_kernel,
        grid=(steps,),
        in_specs=[pl.BlockSpec((tile, 1), lambda i: (i, 0)), pl.BlockSpec((1, LANES), lambda i: (0, 0)),
                  wspec],
        out_specs=[tspec] * 5 + [wspec],
        out_shape=[tab] * 5 + [jax.ShapeDtypeStruct((rows, d), BF16)],
        compiler_params=pltpu.CompilerParams(
            dimension_semantics=("arbitrary",), vmem_limit_bytes=VMEM_LIMIT),
        name="rope_tables",
    )(positions.reshape(n, 1), freq, w_in_t)
    return tables, w_in_tb


def _in_proj_kernel(x_ref, gmix_ref, w_ref, gq_ref, wuq_ref, gkv_ref, wukv_ref,
                    cosd_ref, sind_ref, cosm_ref, sinlo_ref, sinhi_ref,
                    q_ref, k_ref, v_ref, dq_ref, dk_ref, dv_ref, h_ref,
                    *, mla_scale, diff_scale):
    def project(first_col, width):
        return _nt_dot(h_ref[...], w_ref[first_col:first_col + width, :])

    def rope64(t):
        return (t * cosm_ref[...] + pltpu.roll(t, 96, 1) * sinlo_ref[...]
                + pltpu.roll(t, 32, 1) * sinhi_ref[...])

    def rope128(t):
        return t * cosd_ref[...] + pltpu.roll(t, 64, 1) * sind_ref[...]

    h_ref[...] = _rms(x_ref[...], gmix_ref[...], NORM_EPS).astype(BF16)

    lat = project(0, LAT_DOT_W)
    c_q = _rms(lat[:, :Q_LORA], gq_ref[...], NORM_EPS).astype(BF16)
    q = jnp.dot(c_q, wuq_ref[...], preferred_element_type=F32) * mla_scale
    nope_w = MLA_HEADS * NOPE_DIM
    q_rope = [rope64(q[:, nope_w + c * LANES: nope_w + (c + 1) * LANES]).astype(BF16)
              for c in range(MLA_HEADS // 2)]
    c_kv = _rms(lat[:, Q_LORA:Q_LORA + KV_LORA], gkv_ref[...], NORM_EPS).astype(BF16)
    kv = jnp.dot(c_kv, wukv_ref[...], preferred_element_type=F32)
    kr_slab = lat[:, Q_LORA + KV_LORA:Q_LORA + KV_LORA + LANES]
    kr_lane = lax.broadcasted_iota(jnp.int32, kr_slab.shape, 1)
    kr = rope64(jnp.where(kr_lane < ROPE_DIM, kr_slab, 0.0))
    kr_even, kr_odd = kr.astype(BF16), pltpu.roll(kr, 64, 1).astype(BF16)
    ones_col = jnp.where(kr_lane == 0, 1.0, 0.0).astype(BF16)
    for h in range(MLA_HEADS):
        base = h * MLA_QK
        q_ref[:, base:base + NOPE_DIM] = q[:, h * NOPE_DIM:(h + 1) * NOPE_DIM].astype(BF16)
        q_ref[:, base + NOPE_DIM:base + MLA_QK] = q_rope[h // 2]
        k_ref[:, base:base + NOPE_DIM] = kv[:, base:base + NOPE_DIM].astype(BF16)
        k_ref[:, base + NOPE_DIM:base + MLA_QK] = kr_even if h % 2 == 0 else kr_odd
        v_ref[:, h * V_EXT:h * V_EXT + V_DIM] = kv[:, base + NOPE_DIM:base + MLA_QK].astype(BF16)
        v_ref[:, h * V_EXT + V_DIM:(h + 1) * V_EXT] = ones_col

    dq = project(LAT_W, DIFF_W)
    for c in range(dq.shape[1] // LANES):
        sl = slice(c * LANES, (c + 1) * LANES)
        dq_ref[:, sl] = (rope128(dq[:, sl]) * diff_scale).astype(BF16)
    dk = project(LAT_W + DIFF_W, DIFF_W)
    for c in range(dk.shape[1] // LANES):
        sl = slice(c * LANES, (c + 1) * LANES)
        dk_ref[:, sl] = rope128(dk[:, sl]).astype(BF16)
    dv_ref[...] = project(LAT_W + 2 * DIFF_W, DIFF_W).astype(BF16)


def _in_proj(x2d, g_mix, w_in_b, g_q, w_uq_b, g_kv, w_ukv_b, tables, tm):
    n, d = x2d.shape
    row = lambda i: (i, 0)
    const = lambda i: (0, 0)
    resident = lambda shape: pl.BlockSpec(shape, const, pipeline_mode=pl.Buffered(1))
    kern = functools.partial(
        _in_proj_kernel,
        mla_scale=(NOPE_DIM + ROPE_DIM) ** -0.5 * LOG2E,
        diff_scale=DIFF_DIM ** -0.5 * LOG2E)
    outs = [
        jax.ShapeDtypeStruct((n, MLA_HEADS * MLA_QK), BF16),
        jax.ShapeDtypeStruct((n, MLA_HEADS * MLA_QK), BF16),
        jax.ShapeDtypeStruct((n, MLA_HEADS * V_EXT), BF16),
        jax.ShapeDtypeStruct((n, DIFF_W), BF16),
        jax.ShapeDtypeStruct((n, DIFF_W), BF16),
        jax.ShapeDtypeStruct((n, DIFF_W), BF16),
    ]
    return pl.pallas_call(
        kern,
        grid=(n // tm,),
        in_specs=[
            pl.BlockSpec((tm, d), row),
            pl.BlockSpec((1, d), const),
            resident(w_in_b.shape),
            pl.BlockSpec((1, Q_LORA), const),
            resident(w_uq_b.shape),
            pl.BlockSpec((1, KV_LORA), const),
            resident(w_ukv_b.shape),
        ] + [pl.BlockSpec((tm, LANES), row)] * 5,
        out_specs=[pl.BlockSpec((tm, o.shape[1]), row) for o in outs],
        out_shape=outs,
        scratch_shapes=[pltpu.VMEM((tm, d), BF16)],
        compiler_params=pltpu.CompilerParams(
            dimension_semantics=("parallel",), vmem_limit_bytes=VMEM_LIMIT),
        name="in_proj",
    )(x2d, g_mix, w_in_b, g_q, w_uq_b, g_kv, w_ukv_b, *tables)


def _cast_slab_specs(weights, n_steps, step_of):
    in_specs, out_specs, out_shapes = [], [], []
    for w in weights:
        slab = w.shape[0] // n_steps
        assert slab * n_steps == w.shape[0] and slab % 16 == 0, (w.shape, n_steps)
        spec = pl.BlockSpec((slab, w.shape[1]), lambda *ids: (step_of(*ids), 0))
        in_specs.append(spec)
        out_specs.append(spec)
        out_shapes.append(jax.ShapeDtypeStruct(w.shape, BF16))
    return in_specs, out_specs, out_shapes


def _cast_slabs(src_refs, dst_refs):
    for src, dst in zip(src_refs, dst_refs):
        dst[...] = src[...].astype(BF16)


def _mla_attn_kernel(q_ref, k_ref, v_ref, *refs, tq, n_cast):
    cast_src, (o_ref, *cast_dst) = refs[:n_cast], refs[n_cast:2 * n_cast + 1]
    s0_ref, s1_ref, p0_ref, p1_ref = refs[2 * n_cast + 1:]
    _cast_slabs(cast_src, cast_dst)
    nq = q_ref.shape[0] // tq
    s_bufs, p_bufs = (s0_ref, s1_ref), (p0_ref, p1_ref)
    rows = lambda t: slice(t * tq, (t + 1) * tq)

    def scores(t):
        s_bufs[t % 2][...] = _nt_dot(q_ref[rows(t), :], k_ref[...])

    def softmax(t):
        s = s_bufs[t % 2][...]
        p_bufs[t % 2][...] = jnp.exp2(s - jnp.max(s, axis=-1, keepdims=True)).astype(BF16)

    def pv(t):
        o = jnp.dot(p_bufs[t % 2][...], v_ref[...], preferred_element_type=F32)
        o_ref[rows(t), :] = (o[:, :V_DIM] * (1.0 / o[:, V_DIM:V_DIM + 1])).astype(o_ref.dtype)

    scores(0)
    for t in range(nq):
        if t + 1 < nq:
            scores(t + 1)
        softmax(t)
        if t >= 1:
            pv(t - 1)
    pv(nq - 1)


def _mla_attn(q, k, v, cast_weights, batch, seq, tq):
    blk = lambda w: pl.BlockSpec((seq, w), lambda b, h: (b, h))
    cast_in, cast_out, cast_shapes = _cast_slab_specs(
        cast_weights, batch * MLA_HEADS, lambda b, h: b * MLA_HEADS + h)
    return pl.pallas_call(
        functools.partial(_mla_attn_kernel, tq=tq, n_cast=len(cast_weights)),
        grid=(batch, MLA_HEADS),
        in_specs=[blk(MLA_QK), blk(MLA_QK), blk(V_EXT)] + cast_in,
        out_specs=[blk(V_DIM)] + cast_out,
        out_shape=[jax.ShapeDtypeStruct((batch * seq, MLA_HEADS * V_DIM), BF16)] + cast_shapes,
        scratch_shapes=[pltpu.VMEM((tq, seq), F32)] * 2 + [pltpu.VMEM((tq, seq), BF16)] * 2,
        compiler_params=pltpu.CompilerParams(
            dimension_semantics=("parallel", "parallel"), vmem_limit_bytes=VMEM_LIMIT),
        name="mla_attn",
    )(q, k, v, *cast_weights)


def _diff_attn_kernel(lq1_ref, lk1_ref, lq2_ref, lk2_ref, q_ref, k_ref, v_ref, g_ref, *refs,
                      tq, lambda_init, n_cast):
    cast_src, (o_ref, *cast_dst) = refs[:n_cast], refs[n_cast:2 * n_cast + 1]
    sa0_ref, sa1_ref, sb0_ref, sb1_ref, p0_ref, p1_ref = refs[2 * n_cast + 1:]
    _cast_slabs(cast_src, cast_dst)
    lam = (jnp.exp(jnp.sum(lq1_ref[...] * lk1_ref[...], axis=-1, keepdims=True))
           - jnp.exp(jnp.sum(lq2_ref[...] * lk2_ref[...], axis=-1, keepdims=True))
           + lambda_init)
    nq = q_ref.shape[0] // tq
    sa_bufs, sb_bufs, p_bufs, inv_l1 = (sa0_ref, sa1_ref), (sb0_ref, sb1_ref), (p0_ref, p1_ref), {}
    rows = lambda t: slice(t * tq, (t + 1) * tq)

    def scores(t):
        sa_bufs[t % 2][...] = _nt_dot(q_ref[rows(t), :DIFF_DIM], k_ref[:, :DIFF_DIM])
        sb_bufs[t % 2][...] = _nt_dot(q_ref[rows(t), DIFF_DIM:], k_ref[:, DIFF_DIM:])

    def softmax(t):
        e1, l1 = _softmax_parts(sa_bufs[t % 2][...])
        e2, l2 = _softmax_parts(sb_bufs[t % 2][...])
        p_bufs[t % 2][...] = (e1 - e2 * (lam * l1 / l2)).astype(BF16)
        inv_l1[t] = 1.0 / l1

    def pv(t):
        o = jnp.dot(p_bufs[t % 2][...], v_ref[...], preferred_element_type=F32) * inv_l1.pop(t)
        o = _rms(o, g_ref[...], DIFF_NORM_EPS) * (1.0 - lambda_init)
        o_ref[rows(t), :] = o.astype(o_ref.dtype)

    scores(0)
    for t in range(nq):
        if t + 1 < nq:
            scores(t + 1)
        softmax(t)
        if t >= 1:
            pv(t - 1)
    pv(nq - 1)


def _diff_attn(lams, dq, dk, dv, g_sub, cast_weights, batch, seq, tq, lambda_init):
    w = 2 * DIFF_DIM
    blk = pl.BlockSpec((seq, w), lambda b, h: (b, h))
    vec = lambda width: pl.BlockSpec((1, width), lambda b, h: (0, 0))
    cast_in, cast_out, cast_shapes = _cast_slab_specs(
        cast_weights, batch * DIFF_HEADS, lambda b, h: b * DIFF_HEADS + h)
    return pl.pallas_call(
        functools.partial(_diff_attn_kernel, tq=tq, lambda_init=lambda_init, n_cast=len(cast_weights)),
        grid=(batch, DIFF_HEADS),
        in_specs=[vec(DIFF_DIM)] * 4 + [blk, blk, blk, vec(w)] + cast_in,
        out_specs=[blk] + cast_out,
        out_shape=[jax.ShapeDtypeStruct(dv.shape, BF16)] + cast_shapes,
        scratch_shapes=[pltpu.VMEM((tq, seq), F32)] * 4 + [pltpu.VMEM((tq, seq), BF16)] * 2,
        compiler_params=pltpu.CompilerParams(
            dimension_semantics=("parallel", "parallel"), vmem_limit_bytes=VMEM_LIMIT),
        name="diff_attn",
    )(*lams, dq, dk, dv, g_sub, *cast_weights)


def _mem_kv_kernel(mem_ref, g_ref, wk_ref, wv_ref, k_ref, v_ref):
    hm = _rms(mem_ref[...], g_ref[...], NORM_EPS).astype(BF16)
    k_ref[...] = jnp.dot(hm, wk_ref[...], preferred_element_type=F32).astype(BF16)
    v_ref[...] = jnp.dot(hm, wv_ref[...], preferred_element_type=F32).astype(BF16)


def _mem_kv(mem2d, g_mem, w_xk_b, w_xv_b, m_len):
    n, d = mem2d.shape
    xw = w_xk_b.shape[1]
    const = lambda b: (0, 0)
    out = jax.ShapeDtypeStruct((n, xw), BF16)
    return pl.pallas_call(
        _mem_kv_kernel,
        grid=(n // m_len,),
        in_specs=[pl.BlockSpec((m_len, d), lambda b: (b, 0)), pl.BlockSpec((1, d), const),
                  pl.BlockSpec((d, xw), const), pl.BlockSpec((d, xw), const)],
        out_specs=[pl.BlockSpec((m_len, xw), lambda b: (b, 0))] * 2,
        out_shape=[out, out],
        compiler_params=pltpu.CompilerParams(
            dimension_semantics=("parallel",), vmem_limit_bytes=VMEM_LIMIT),
        name="mem_kv",
    )(mem2d, g_mem, w_xk_b, w_xv_b)


def _out_xattn_kernel(x_ref, a_ref, b_ref, wo_ref, gx_ref, wxq_ref, xk_ref, xv_ref, wxo_ref, o_ref,
                      *, x_scale):
    half = a_ref.shape[1]
    x1 = (x_ref[...]
          + jnp.dot(a_ref[...], wo_ref[:half, :], preferred_element_type=F32)
          + jnp.dot(b_ref[...], wo_ref[half:, :], preferred_element_type=F32))
    hx = _rms(x1, gx_ref[...], NORM_EPS).astype(BF16)
    xq = (jnp.dot(hx, wxq_ref[...], preferred_element_type=F32) * x_scale).astype(BF16)
    heads = []
    for h in range(X_HEADS):
        sl = slice(h * X_DIM, (h + 1) * X_DIM)
        e, l = _softmax_parts(_nt_dot(xq[:, sl], xk_ref[:, sl]))
        o = jnp.dot(e.astype(BF16), xv_ref[:, sl], preferred_element_type=F32)
        heads.append((o * (1.0 / l)).astype(BF16))
    xo = jnp.concatenate(heads, axis=-1)
    o_ref[...] = x1 + jnp.dot(xo, wxo_ref[...], preferred_element_type=F32)


def _out_xattn(x2d, a, b, w_out_b, g_x, w_xq_b, xk, xv, w_xo_b, seq, m_len, tm):
    n, d = x2d.shape
    xw = w_xq_b.shape[1]
    per_b = seq // tm
    row = lambda i: (i, 0)
    const = lambda i: (0, 0)
    mem = lambda i: (i // per_b, 0)
    return pl.pallas_call(
        functools.partial(_out_xattn_kernel, x_scale=X_DIM ** -0.5 * LOG2E),
        grid=(n // tm,),
        in_specs=[
            pl.BlockSpec((tm, d), row),
            pl.BlockSpec((tm, a.shape[1]), row),
            pl.BlockSpec((tm, b.shape[1]), row),
            pl.BlockSpec(w_out_b.shape, const),
            pl.BlockSpec((1, d), const),
            pl.BlockSpec(w_xq_b.shape, const),
            pl.BlockSpec((m_len, xw), mem),
            pl.BlockSpec((m_len, xw), mem),
            pl.BlockSpec(w_xo_b.shape, const),
        ],
        out_specs=pl.BlockSpec((tm, d), row),
        out_shape=jax.ShapeDtypeStruct((n, d), F32),
        compiler_params=pltpu.CompilerParams(
            dimension_semantics=("parallel",), vmem_limit_bytes=VMEM_LIMIT),
        name="out_xattn",
    )(x2d, a, b, w_out_b, g_x, w_xq_b, xk, xv, w_xo_b)


def _ffn_kernel(x_ref, g_ref, wg_ref, wu_ref, wd_ref, gf_ref, o_ref, h_ref, *, final_norm, row_split):
    j = pl.program_id(1)
    nj = pl.num_programs(1)
    tr = x_ref.shape[0] // row_split

    def step(first, last):
        for r in range(row_split):
            rows = slice(r * tr, (r + 1) * tr)
            if first:
                h_ref[rows, :] = _rms(x_ref[rows, :], g_ref[...], NORM_EPS).astype(BF16)
            h = h_ref[rows, :]
            gate = jnp.dot(h, wg_ref[...], preferred_element_type=F32)
            up = jnp.dot(h, wu_ref[...], preferred_element_type=F32)
            act = (gate * jax.nn.sigmoid(gate) * up).astype(BF16)
            y = (x_ref[rows, :] if first else o_ref[rows, :]) + jnp.dot(
                act, wd_ref[...], preferred_element_type=F32)
            if last and final_norm:
                y = _rms(y, gf_ref[...], NORM_EPS)
            o_ref[rows, :] = y

    pl.when(j == 0)(lambda: step(True, False))
    pl.when((j > 0) & (j < nj - 1))(lambda: step(False, False))
    pl.when(j == nj - 1)(lambda: step(False, True))


def _ffn(x2d, g_ffn, w_gate_b, w_up_b, w_down_b, g_final, tm, th, final_norm):
    n, d = x2d.shape
    hidden = w_gate_b.shape[1]
    assert hidden // th >= 2
    row = lambda i, j: (i, 0)
    const = lambda i, j: (0, 0)
    return pl.pallas_call(
        functools.partial(_ffn_kernel, final_norm=final_norm, row_split=2),
        grid=(n // tm, hidden // th),
        in_specs=[
            pl.BlockSpec((tm, d), row),
            pl.BlockSpec((1, d), const),
            pl.BlockSpec((d, th), lambda i, j: (0, j)),
            pl.BlockSpec((d, th), lambda i, j: (0, j)),
            pl.BlockSpec((th, d), lambda i, j: (j, 0)),
            pl.BlockSpec((1, d), const),
        ],
        out_specs=pl.BlockSpec((tm, d), row),
        out_shape=jax.ShapeDtypeStruct((n, d), F32),
        scratch_shapes=[pltpu.VMEM((tm, d), BF16)],
        compiler_params=pltpu.CompilerParams(
            dimension_semantics=("parallel", "arbitrary"), vmem_limit_bytes=VMEM_LIMIT),
        name="ffn",
    )(x2d, g_ffn, w_gate_b, w_up_b, w_down_b, g_final)


def kernel(x, mem, positions, g_mix, w_in, g_q_lat, w_uq, g_kv_lat, w_ukv, lambda_q1, lambda_k1,
           lambda_q2, lambda_k2, g_diff_sub, w_out, g_xattn, g_mem, w_xq, w_xk, w_xv, w_xo,
           g_ffn, w_gate, w_up, w_down, g_final):
    batch, seq, d = x.shape
    m_len = mem.shape[1]
    depth = w_in.shape[0]
    n = batch * seq

    head_w = NOPE_DIM + ROPE_DIM
    uq_cols = np.concatenate(
        [h * head_w + np.arange(NOPE_DIM) for h in range(MLA_HEADS)]
        + [h * head_w + NOPE_DIM + np.arange(ROPE_DIM) for h in range(MLA_HEADS)])

    x2d = x.reshape(n, d)
    mem2d = mem.reshape(batch * m_len, d)
    row = lambda v: v.reshape(1, -1)

    for layer in range(depth):
        lambda_init = 0.8 - 0.6 * math.exp(-0.3 * layer)
        tables, w_in_b = _rope_tables(positions, jnp.swapaxes(w_in[layer], 0, 1), tile=2048)
        q, k, v, dq, dk, dv = _in_proj(
            x2d, row(g_mix[layer]), w_in_b, row(g_q_lat[layer]),
            w_uq[layer][:, uq_cols].astype(BF16), row(g_kv_lat[layer]), w_ukv[layer].astype(BF16),
            tables, tm=512)
        out_mla, w_gate_b, w_up_b = _mla_attn(
            q, k, v, [w_gate[layer], w_up[layer]], batch, seq, tq=512)
        out_diff, w_down_b, w_out_b, w_xq_b, w_xk_b, w_xv_b, w_xo_b = _diff_attn(
            [row(lambda_q1[layer]), row(lambda_k1[layer]), row(lambda_q2[layer]), row(lambda_k2[layer])],
            dq, dk, dv, row(g_diff_sub[layer]),
            [w_down[layer], w_out[layer], w_xq[layer], w_xk[layer], w_xv[layer], w_xo[layer]],
            batch, seq, tq=512, lambda_init=lambda_init)
        xk, xv = _mem_kv(mem2d, row(g_mem[layer]), w_xk_b, w_xv_b, m_len)
        x2d = _out_xattn(x2d, out_mla, out_diff, w_out_b, row(g_xattn[layer]),
                         w_xq_b, xk, xv, w_xo_b, seq, m_len, tm=512)
        x2d = _ffn(x2d, row(g_ffn[layer]), w_gate_b, w_up_b, w_down_b, row(g_final),
                   tm=1024, th=512, final_norm=(layer == depth - 1))
    return x2d.reshape(batch, seq, d)
```

```python
import functools
import math

import jax
import jax.numpy as jnp
from jax import lax
from jax.experimental import pallas as pl
from jax.experimental.pallas import tpu as pltpu

F32 = jnp.float32
BF16 = jnp.bfloat16

ROPE_THETA = 10000.0
NORM_EPS = 1e-6
DIFF_NORM_EPS = 1e-5
Q_LORA, KV_LORA, ROPE_DIM = 512, 256, 64
MLA_HEADS, NOPE_DIM, V_DIM = 8, 128, 128
DIFF_HEADS, DIFF_DIM = 4, 128
X_HEADS, X_DIM = 4, 128
LANES = 128
LOG2E = math.log2(math.e)
VMEM_LIMIT = 58 * 1024 * 1024


class _Tiles:
    in_proj_tokens = 512
    mla_queries = 1024
    diff_queries = 512
    out_xattn_tokens = 512
    ffn_tokens = 1024
    ffn_hidden = 512

LAT_W = Q_LORA + KV_LORA + ROPE_DIM
LAT_DOT_W = Q_LORA + KV_LORA + LANES
DIFF_W = 2 * DIFF_HEADS * DIFF_DIM
MLA_QK = 2 * LANES
V_EXT = 2 * LANES


def _rms(x, g, eps):
    return x * lax.rsqrt(jnp.mean(x * x, axis=-1, keepdims=True) + eps) * g


def _nt_dot(a, b):
    return lax.dot_general(a, b, (((1,), (1,)), ((), ())), preferred_element_type=F32)


def _softmax_parts(s):
    m = jnp.max(s, axis=-1, keepdims=True)
    e = jnp.exp2(s - m)
    return e, jnp.sum(e, axis=-1, keepdims=True)


def _rope_table_kernel(pos_ref, freq_ref, cosd_ref, sind_ref, cosm_ref, sinlo_ref, sinhi_ref):
    ang = pos_ref[...].astype(F32) * freq_ref[...]
    ct, st = jnp.cos(ang), jnp.sin(ang)
    ct_r, st_r = pltpu.roll(ct, 64, 1), pltpu.roll(st, 64, 1)
    lane = lax.broadcasted_iota(jnp.int32, ang.shape, 1)
    lo = lane < 64
    cosd_ref[...] = jnp.where(lo, ct, ct_r)
    sind_ref[...] = jnp.where(lo, -st, st_r)
    cosm_ref[...] = jnp.where(lo, ct_r, ct)
    sm = jnp.where(lo, st_r, st)
    first_half = (lane & 63) < 32
    sinlo_ref[...] = jnp.where(first_half, -sm, 0.0)
    sinhi_ref[...] = jnp.where(first_half, 0.0, sm)


def _prep_kernel(pos_ref, freq_ref, wt_ref, mem_ref, gmem_ref, wxk_ref, wxv_ref, *refs):
    *table_refs, wt_out_ref, xk_ref, xv_ref, wxk_b_ref, wxv_b_ref = refs
    _rope_table_kernel(pos_ref, freq_ref, *table_refs)
    wt_out_ref[...] = wt_ref[...].astype(BF16)

    @pl.when(pl.program_id(0) == 0)
    def _():
        wxk_b_ref[...] = wxk_ref[...].astype(BF16)
        wxv_b_ref[...] = wxv_ref[...].astype(BF16)

    hm = _rms(mem_ref[...], gmem_ref[...], NORM_EPS).astype(BF16)
    xk_ref[...] = jnp.dot(hm, wxk_b_ref[...], preferred_element_type=F32).astype(BF16)
    xv_ref[...] = jnp.dot(hm, wxv_b_ref[...], preferred_element_type=F32).astype(BF16)


def _prep(positions, w_in_t, mem2d, g_mem, w_xk, w_xv, m_len):
    batch, tile = positions.shape
    n = positions.size
    steps = batch
    rows, d = w_in_t.shape
    xw = w_xk.shape[1]
    n_slabs = max(s for s in range(1, steps + 1) if steps % s == 0 and rows % (16 * s) == 0)
    slab, revisit = rows // n_slabs, steps // n_slabs
    inv_d = ROPE_THETA ** (-jnp.arange(0, DIFF_DIM, 2, dtype=F32) / DIFF_DIM)
    inv_m = ROPE_THETA ** (-jnp.arange(0, ROPE_DIM, 2, dtype=F32) / ROPE_DIM)
    freq = jnp.concatenate([inv_d, inv_m, inv_m]).reshape(1, LANES)
    tab = jax.ShapeDtypeStruct((n, LANES), F32)
    tspec = pl.BlockSpec((tile, LANES), lambda i: (i, 0))
    wspec = pl.BlockSpec((slab, d), lambda i: (i // revisit, 0))
    const = lambda i: (0, 0)
    mem_out = pl.BlockSpec((m_len, xw), lambda i: (i, 0))
    xkv = jax.ShapeDtypeStruct((batch * m_len, xw), BF16)
    *tables, w_in_tb, xk, xv = pl.pallas_call(
        _prep_kernel,
        grid=(steps,),
        in_specs=[pl.BlockSpec((tile, 1), lambda i: (i, 0)), pl.BlockSpec((1, LANES), const), wspec,
                  pl.BlockSpec((m_len, d), lambda i: (i, 0)), pl.BlockSpec((1, d), const),
                  pl.BlockSpec(w_xk.shape, const, pipeline_mode=pl.Buffered(1)),
                  pl.BlockSpec(w_xv.shape, const, pipeline_mode=pl.Buffered(1))],
        out_specs=[tspec] * 5 + [wspec, mem_out, mem_out],
        out_shape=[tab] * 5 + [jax.ShapeDtypeStruct((rows, d), BF16), xkv, xkv],
        scratch_shapes=[pltpu.VMEM(w_xk.shape, BF16), pltpu.VMEM(w_xv.shape, BF16)],
        compiler_params=pltpu.CompilerParams(
            dimension_semantics=("arbitrary",), vmem_limit_bytes=VMEM_LIMIT),
        name="prep",
    )(positions.reshape(n, 1), freq, w_in_t, mem2d, g_mem, w_xk, w_xv)
    return tables, w_in_tb, xk, xv


def _in_proj_kernel(x_ref, gmix_ref, w_ref, gq_ref, wuq_ref, gkv_ref, wukv_ref,
                    cosd_ref, sind_ref, cosm_ref, sinlo_ref, sinhi_ref,
                    q_ref, k_ref, v_ref, dq_ref, dk_ref, dv_ref, h_ref, wuq_b_ref, wukv_b_ref,
                    *, mla_scale, diff_scale):
    @pl.when(pl.program_id(0) == 0)
    def _():
        head_w, nope_w = NOPE_DIM + ROPE_DIM, MLA_HEADS * NOPE_DIM
        for h in range(MLA_HEADS):
            wuq_b_ref[:, h * NOPE_DIM:(h + 1) * NOPE_DIM] = (
                wuq_ref[:, h * head_w:h * head_w + NOPE_DIM].astype(BF16))
            wuq_b_ref[:, nope_w + h * ROPE_DIM:nope_w + (h + 1) * ROPE_DIM] = (
                wuq_ref[:, h * head_w + NOPE_DIM:(h + 1) * head_w].astype(BF16))
        wukv_b_ref[...] = wukv_ref[...].astype(BF16)

    def project(first_col, width):
        return _nt_dot(h_ref[...], w_ref[first_col:first_col + width, :])

    def rope64(t):
        return (t * cosm_ref[...] + pltpu.roll(t, 96, 1) * sinlo_ref[...]
                + pltpu.roll(t, 32, 1) * sinhi_ref[...])

    def rope128(t):
        return t * cosd_ref[...] + pltpu.roll(t, 64, 1) * sind_ref[...]

    h_ref[...] = _rms(x_ref[...], gmix_ref[...], NORM_EPS).astype(BF16)

    lat = project(0, LAT_DOT_W)
    c_q = _rms(lat[:, :Q_LORA], gq_ref[...], NORM_EPS).astype(BF16)
    q = jnp.dot(c_q, wuq_b_ref[...], preferred_element_type=F32) * mla_scale
    nope_w = MLA_HEADS * NOPE_DIM
    q_rope = [rope64(q[:, nope_w + c * LANES: nope_w + (c + 1) * LANES]).astype(BF16)
              for c in range(MLA_HEADS // 2)]
    c_kv = _rms(lat[:, Q_LORA:Q_LORA + KV_LORA], gkv_ref[...], NORM_EPS).astype(BF16)
    kv = jnp.dot(c_kv, wukv_b_ref[...], preferred_element_type=F32)
    kr_slab = lat[:, Q_LORA + KV_LORA:Q_LORA + KV_LORA + LANES]
    kr_lane = lax.broadcasted_iota(jnp.int32, kr_slab.shape, 1)
    kr = rope64(jnp.where(kr_lane < ROPE_DIM, kr_slab, 0.0))
    kr_even, kr_odd = kr.astype(BF16), pltpu.roll(kr, 64, 1).astype(BF16)
    ones_col = jnp.where(kr_lane == 0, 1.0, 0.0).astype(BF16)
    for h in range(MLA_HEADS):
        base = h * MLA_QK
        q_ref[:, base:base + NOPE_DIM] = q[:, h * NOPE_DIM:(h + 1) * NOPE_DIM].astype(BF16)
        q_ref[:, base + NOPE_DIM:base + MLA_QK] = q_rope[h // 2]
        k_ref[:, base:base + NOPE_DIM] = kv[:, base:base + NOPE_DIM].astype(BF16)
        k_ref[:, base + NOPE_DIM:base + MLA_QK] = kr_even if h % 2 == 0 else kr_odd
        v_ref[:, h * V_EXT:h * V_EXT + V_DIM] = kv[:, base + NOPE_DIM:base + MLA_QK].astype(BF16)
        v_ref[:, h * V_EXT + V_DIM:(h + 1) * V_EXT] = ones_col

    dq = project(LAT_W, DIFF_W)
    for c in range(dq.shape[1] // LANES):
        sl = slice(c * LANES, (c + 1) * LANES)
        dq_ref[:, sl] = (rope128(dq[:, sl]) * diff_scale).astype(BF16)
    dk = project(LAT_W + DIFF_W, DIFF_W)
    for c in range(dk.shape[1] // LANES):
        sl = slice(c * LANES, (c + 1) * LANES)
        dk_ref[:, sl] = rope128(dk[:, sl]).astype(BF16)
    dv_ref[...] = project(LAT_W + 2 * DIFF_W, DIFF_W).astype(BF16)


def _in_proj(x2d, g_mix, w_in_b, g_q, w_uq, g_kv, w_ukv, tables, tm):
    n, d = x2d.shape
    row = lambda i: (i, 0)
    const = lambda i: (0, 0)
    resident = lambda shape: pl.BlockSpec(shape, const, pipeline_mode=pl.Buffered(1))
    kern = functools.partial(
        _in_proj_kernel,
        mla_scale=(NOPE_DIM + ROPE_DIM) ** -0.5 * LOG2E,
        diff_scale=DIFF_DIM ** -0.5 * LOG2E)
    outs = [
        jax.ShapeDtypeStruct((n, MLA_HEADS * MLA_QK), BF16),
        jax.ShapeDtypeStruct((n, MLA_HEADS * MLA_QK), BF16),
        jax.ShapeDtypeStruct((n, MLA_HEADS * V_EXT), BF16),
        jax.ShapeDtypeStruct((n, DIFF_W), BF16),
        jax.ShapeDtypeStruct((n, DIFF_W), BF16),
        jax.ShapeDtypeStruct((n, DIFF_W), BF16),
    ]
    return pl.pallas_call(
        kern,
        grid=(n // tm,),
        in_specs=[
            pl.BlockSpec((tm, d), row),
            pl.BlockSpec((1, d), const),
            resident(w_in_b.shape),
            pl.BlockSpec((1, Q_LORA), const),
            resident(w_uq.shape),
            pl.BlockSpec((1, KV_LORA), const),
            resident(w_ukv.shape),
        ] + [pl.BlockSpec((tm, LANES), row)] * 5,
        out_specs=[pl.BlockSpec((tm, o.shape[1]), row) for o in outs],
        out_shape=outs,
        scratch_shapes=[pltpu.VMEM((tm, d), BF16), pltpu.VMEM(w_uq.shape, BF16), pltpu.VMEM(w_ukv.shape, BF16)],
        compiler_params=pltpu.CompilerParams(
            dimension_semantics=("arbitrary",), vmem_limit_bytes=VMEM_LIMIT),
        name="in_proj",
    )(x2d, g_mix, w_in_b, g_q, w_uq, g_kv, w_ukv, *tables)


def _cast_slab_specs(weights, n_steps, step_of):
    in_specs, out_specs, out_shapes = [], [], []
    for w in weights:
        slab = w.shape[0] // n_steps
        assert slab * n_steps == w.shape[0] and slab % 16 == 0, (w.shape, n_steps)
        spec = pl.BlockSpec((slab, w.shape[1]), lambda *ids: (step_of(*ids), 0))
        in_specs.append(spec)
        out_specs.append(spec)
        out_shapes.append(jax.ShapeDtypeStruct(w.shape, BF16))
    return in_specs, out_specs, out_shapes


def _cast_slabs(src_refs, dst_refs):
    for src, dst in zip(src_refs, dst_refs):
        dst[...] = src[...].astype(BF16)


def _mla_attn_kernel(q_ref, k_ref, v_ref, *refs, tq, n_cast):
    cast_src, (o_ref, *cast_dst) = refs[:n_cast], refs[n_cast:2 * n_cast + 1]
    s0_ref, s1_ref, p0_ref, p1_ref = refs[2 * n_cast + 1:]
    _cast_slabs(cast_src, cast_dst)
    nq = q_ref.shape[0] // tq
    s_bufs, p_bufs = (s0_ref, s1_ref), (p0_ref, p1_ref)
    rows = lambda t: slice(t * tq, (t + 1) * tq)

    def scores(t):
        s_bufs[t % 2][...] = _nt_dot(q_ref[rows(t), :], k_ref[...])

    def softmax(t):
        s = s_bufs[t % 2][...]
        p_bufs[t % 2][...] = jnp.exp2(s - jnp.max(s, axis=-1, keepdims=True)).astype(BF16)

    def pv(t):
        o = jnp.dot(p_bufs[t % 2][...], v_ref[...], preferred_element_type=F32)
        o_ref[rows(t), :] = (o[:, :V_DIM] * (1.0 / o[:, V_DIM:V_DIM + 1])).astype(o_ref.dtype)

    scores(0)
    for t in range(nq):
        if t + 1 < nq:
            scores(t + 1)
        softmax(t)
        if t >= 1:
            pv(t - 1)
    pv(nq - 1)


def _mla_attn(q, k, v, cast_weights, batch, seq, tq):
    blk = lambda w: pl.BlockSpec((seq, w), lambda b, h: (b, h))
    cast_in, cast_out, cast_shapes = _cast_slab_specs(
        cast_weights, batch * MLA_HEADS, lambda b, h: b * MLA_HEADS + h)
    return pl.pallas_call(
        functools.partial(_mla_attn_kernel, tq=tq, n_cast=len(cast_weights)),
        grid=(batch, MLA_HEADS),
        in_specs=[blk(MLA_QK), blk(MLA_QK), blk(V_EXT)] + cast_in,
        out_specs=[blk(V_DIM)] + cast_out,
        out_shape=[jax.ShapeDtypeStruct((batch * seq, MLA_HEADS * V_DIM), BF16)] + cast_shapes,
        scratch_shapes=[pltpu.VMEM((tq, seq), F32)] * 2 + [pltpu.VMEM((tq, seq), BF16)] * 2,
        compiler_params=pltpu.CompilerParams(
            dimension_semantics=("parallel", "parallel"), vmem_limit_bytes=VMEM_LIMIT),
        name="mla_attn",
    )(q, k, v, *cast_weights)


def _diff_attn_kernel(lq1_ref, lk1_ref, lq2_ref, lk2_ref, q_ref, k_ref, v_ref, g_ref, *refs,
                      tq, lambda_init, n_cast):
    cast_src, (o_ref, *cast_dst) = refs[:n_cast], refs[n_cast:2 * n_cast + 1]
    sa0_ref, sa1_ref, sb0_ref, sb1_ref, p0_ref, p1_ref = refs[2 * n_cast + 1:]
    _cast_slabs(cast_src, cast_dst)
    lam = (jnp.exp(jnp.sum(lq1_ref[...] * lk1_ref[...], axis=-1, keepdims=True))
           - jnp.exp(jnp.sum(lq2_ref[...] * lk2_ref[...], axis=-1, keepdims=True))
           + lambda_init)
    nq = q_ref.shape[0] // tq
    sa_bufs, sb_bufs, p_bufs, inv_l1 = (sa0_ref, sa1_ref), (sb0_ref, sb1_ref), (p0_ref, p1_ref), {}
    rows = lambda t: slice(t * tq, (t + 1) * tq)

    def scores(t):
        sa_bufs[t % 2][...] = _nt_dot(q_ref[rows(t), :DIFF_DIM], k_ref[:, :DIFF_DIM])
        sb_bufs[t % 2][...] = _nt_dot(q_ref[rows(t), DIFF_DIM:], k_ref[:, DIFF_DIM:])

    def softmax(t):
        e1, l1 = _softmax_parts(sa_bufs[t % 2][...])
        e2, l2 = _softmax_parts(sb_bufs[t % 2][...])
        p_bufs[t % 2][...] = (e1 - e2 * (lam * l1 / l2)).astype(BF16)
        inv_l1[t] = 1.0 / l1

    def pv(t):
        o = jnp.dot(p_bufs[t % 2][...], v_ref[...], preferred_element_type=F32) * inv_l1.pop(t)
        o = _rms(o, g_ref[...], DIFF_NORM_EPS) * (1.0 - lambda_init)
        o_ref[rows(t), :] = o.astype(o_ref.dtype)

    scores(0)
    for t in range(nq):
        if t + 1 < nq:
            scores(t + 1)
        softmax(t)
        if t >= 1:
            pv(t - 1)
    pv(nq - 1)


def _diff_attn(lams, dq, dk, dv, g_sub, cast_weights, batch, seq, tq, lambda_init):
    w = 2 * DIFF_DIM
    blk = pl.BlockSpec((seq, w), lambda b, h: (b, h))
    vec = lambda width: pl.BlockSpec((1, width), lambda b, h: (0, 0))
    cast_in, cast_out, cast_shapes = _cast_slab_specs(
        cast_weights, batch * DIFF_HEADS, lambda b, h: b * DIFF_HEADS + h)
    return pl.pallas_call(
        functools.partial(_diff_attn_kernel, tq=tq, lambda_init=lambda_init, n_cast=len(cast_weights)),
        grid=(batch, DIFF_HEADS),
        in_specs=[vec(DIFF_DIM)] * 4 + [blk, blk, blk, vec(w)] + cast_in,
        out_specs=[blk] + cast_out,
        out_shape=[jax.ShapeDtypeStruct(dv.shape, BF16)] + cast_shapes,
        scratch_shapes=[pltpu.VMEM((tq, seq), F32)] * 4 + [pltpu.VMEM((tq, seq), BF16)] * 2,
        compiler_params=pltpu.CompilerParams(
            dimension_semantics=("parallel", "parallel"), vmem_limit_bytes=VMEM_LIMIT),
        name="diff_attn",
    )(*lams, dq, dk, dv, g_sub, *cast_weights)


def _out_xattn_kernel(x_ref, a_ref, b_ref, wo_ref, gx_ref, wxq_ref, xk_ref, xv_ref, wxo_ref, o_ref,
                      *, x_scale):
    half = a_ref.shape[1]
    x1 = (x_ref[...]
          + jnp.dot(a_ref[...], wo_ref[:half, :], preferred_element_type=F32)
          + jnp.dot(b_ref[...], wo_ref[half:, :], preferred_element_type=F32))
    hx = _rms(x1, gx_ref[...], NORM_EPS).astype(BF16)
    xq = (jnp.dot(hx, wxq_ref[...], preferred_element_type=F32) * x_scale).astype(BF16)
    heads = []
    for h in range(X_HEADS):
        sl = slice(h * X_DIM, (h + 1) * X_DIM)
        e, l = _softmax_parts(_nt_dot(xq[:, sl], xk_ref[:, sl]))
        o = jnp.dot(e.astype(BF16), xv_ref[:, sl], preferred_element_type=F32)
        heads.append((o * (1.0 / l)).astype(BF16))
    xo = jnp.concatenate(heads, axis=-1)
    o_ref[...] = x1 + jnp.dot(xo, wxo_ref[...], preferred_element_type=F32)


def _out_xattn(x2d, a, b, w_out_b, g_x, w_xq_b, xk, xv, w_xo_b, seq, m_len, tm):
    n, d = x2d.shape
    xw = w_xq_b.shape[1]
    per_b = seq // tm
    row = lambda i: (i, 0)
    const = lambda i: (0, 0)
    mem = lambda i: (i // per_b, 0)
    return pl.pallas_call(
        functools.partial(_out_xattn_kernel, x_scale=X_DIM ** -0.5 * LOG2E),
        grid=(n // tm,),
        in_specs=[
            pl.BlockSpec((tm, d), row),
            pl.BlockSpec((tm, a.shape[1]), row),
            pl.BlockSpec((tm, b.shape[1]), row),
            pl.BlockSpec(w_out_b.shape, const),
            pl.BlockSpec((1, d), const),
            pl.BlockSpec(w_xq_b.shape, const),
            pl.BlockSpec((m_len, xw), mem),
            pl.BlockSpec((m_len, xw), mem),
            pl.BlockSpec(w_xo_b.shape, const),
        ],
        out_specs=pl.BlockSpec((tm, d), row),
        out_shape=jax.ShapeDtypeStruct((n, d), F32),
        compiler_params=pltpu.CompilerParams(
            dimension_semantics=("parallel",), vmem_limit_bytes=VMEM_LIMIT),
        name="out_xattn",
    )(x2d, a, b, w_out_b, g_x, w_xq_b, xk, xv, w_xo_b)


def _ffn_kernel(x_ref, g_ref, wg_ref, wu_ref, wd_ref, gf_ref, o_ref, h_ref, *, final_norm, row_split):
    j = pl.program_id(1)
    nj = pl.num_programs(1)
    tr = x_ref.shape[0] // row_split

    def step(first, last):
        for r in range(row_split):
            rows = slice(r * tr, (r + 1) * tr)
            if first:
                h_ref[rows, :] = _rms(x_ref[rows, :], g_ref[...], NORM_EPS).astype(BF16)
            h = h_ref[rows, :]
            gate = jnp.dot(h, wg_ref[...], preferred_element_type=F32)
            up = jnp.dot(h, wu_ref[...], preferred_element_type=F32)
            act = (gate * jax.nn.sigmoid(gate) * up).astype(BF16)
            y = (x_ref[rows, :] if first else o_ref[rows, :]) + jnp.dot(
                act, wd_ref[...], preferred_element_type=F32)
            if last and final_norm:
                y = _rms(y, gf_ref[...], NORM_EPS)
            o_ref[rows, :] = y

    pl.when(j == 0)(lambda: step(True, False))
    pl.when((j > 0) & (j < nj - 1))(lambda: step(False, False))
    pl.when(j == nj - 1)(lambda: step(False, True))


def _ffn(x2d, g_ffn, w_gate_b, w_up_b, w_down_b, g_final, tm, th, final_norm):
    n, d = x2d.shape
    hidden = w_gate_b.shape[1]
    assert hidden // th >= 2
    row = lambda i, j: (i, 0)
    const = lambda i, j: (0, 0)
    return pl.pallas_call(
        functools.partial(_ffn_kernel, final_norm=final_norm, row_split=2),
        grid=(n // tm, hidden // th),
        in_specs=[
            pl.BlockSpec((tm, d), row),
            pl.BlockSpec((1, d), const),
            pl.BlockSpec((d, th), lambda i, j: (0, j)),
            pl.BlockSpec((d, th), lambda i, j: (0, j)),
            pl.BlockSpec((th, d), lambda i, j: (j, 0)),
            pl.BlockSpec((1, d), const),
        ],
        out_specs=pl.BlockSpec((tm, d), row),
        out_shape=jax.ShapeDtypeStruct((n, d), F32),
        scratch_shapes=[pltpu.VMEM((tm, d), BF16)],
        compiler_params=pltpu.CompilerParams(
            dimension_semantics=("parallel", "arbitrary"), vmem_limit_bytes=VMEM_LIMIT),
        name="ffn",
    )(x2d, g_ffn, w_gate_b, w_up_b, w_down_b, g_final)


def kernel(x, mem, positions, g_mix, w_in, g_q_lat, w_uq, g_kv_lat, w_ukv, lambda_q1, lambda_k1,
           lambda_q2, lambda_k2, g_diff_sub, w_out, g_xattn, g_mem, w_xq, w_xk, w_xv, w_xo,
           g_ffn, w_gate, w_up, w_down, g_final):
    batch, seq, d = x.shape
    m_len = mem.shape[1]
    depth = w_in.shape[0]
    n = batch * seq

    x2d = x.reshape(n, d)
    mem2d = mem.reshape(batch * m_len, d)
    row = lambda v: v.reshape(1, -1)

    for layer in range(depth):
        lambda_init = 0.8 - 0.6 * math.exp(-0.3 * layer)
        tables, w_in_b, xk, xv = _prep(
            positions, jnp.swapaxes(w_in[layer], 0, 1), mem2d, row(g_mem[layer]),
            w_xk[layer], w_xv[layer], m_len)
        q, k, v, dq, dk, dv = _in_proj(
            x2d, row(g_mix[layer]), w_in_b, row(g_q_lat[layer]),
            w_uq[layer], row(g_kv_lat[layer]), w_ukv[layer],
            tables, tm=_Tiles.in_proj_tokens)
        out_mla, w_gate_b, w_up_b = _mla_attn(
            q, k, v, [w_gate[layer], w_up[layer]], batch, seq, tq=_Tiles.mla_queries)
        out_diff, w_down_b, w_out_b, w_xq_b, w_xo_b = _diff_attn(
            [row(lambda_q1[layer]), row(lambda_k1[layer]), row(lambda_q2[layer]), row(lambda_k2[layer])],
            dq, dk, dv, row(g_diff_sub[layer]),
            [w_down[layer], w_out[layer], w_xq[layer], w_xo[layer]],
            batch, seq, tq=_Tiles.diff_queries, lambda_init=lambda_init)
        x2d = _out_xattn(x2d, out_mla, out_diff, w_out_b, row(g_xattn[layer]),
                         w_xq_b, xk, xv, w_xo_b, seq, m_len, tm=_Tiles.out_xattn_tokens)
        x2d = _ffn(x2d, row(g_ffn[layer]), w_gate_b, w_up_b, w_down_b, row(g_final),
                   tm=_Tiles.ffn_tokens, th=_Tiles.ffn_hidden, final_norm=(layer == depth - 1))
    return x2d.reshape(batch, seq, d)
```

```python
import functools
import math

import jax
import jax.numpy as jnp
from jax import lax
from jax.experimental import pallas as pl
from jax.experimental.pallas import tpu as pltpu

F32 = jnp.float32
BF16 = jnp.bfloat16

ROPE_THETA = 10000.0
NORM_EPS = 1e-6
DIFF_NORM_EPS = 1e-5
Q_LORA, KV_LORA, ROPE_DIM = 512, 256, 64
MLA_HEADS, NOPE_DIM, V_DIM = 8, 128, 128
DIFF_HEADS, DIFF_DIM = 4, 128
X_HEADS, X_DIM = 4, 128
LANES = 128
LOG2E = math.log2(math.e)
VMEM_LIMIT = 58 * 1024 * 1024


class _Tiles:
    in_proj_tokens = 512
    mla_queries = 1024
    diff_queries = 512
    out_xattn_tokens = 512
    ffn_tokens = 1024
    ffn_hidden = 512

LAT_W = Q_LORA + KV_LORA + ROPE_DIM
LAT_DOT_W = Q_LORA + KV_LORA + LANES
DIFF_W = 2 * DIFF_HEADS * DIFF_DIM
MLA_QK = 2 * LANES
V_EXT = 2 * LANES


def _rms(x, g, eps):
    return x * lax.rsqrt(jnp.mean(x * x, axis=-1, keepdims=True) + eps) * g


def _nt_dot(a, b):
    return lax.dot_general(a, b, (((1,), (1,)), ((), ())), preferred_element_type=F32)


def _softmax_parts(s):
    m = jnp.max(s, axis=-1, keepdims=True)
    e = jnp.exp2(s - m)
    return e, jnp.sum(e, axis=-1, keepdims=True)


def _rope_table_kernel(pos_ref, freq_ref, cosd_ref, sind_ref, cosm_ref, sinlo_ref, sinhi_ref):
    pos_row = pos_ref[0].astype(F32)
    ang = jnp.broadcast_to(pos_row, (LANES, pos_row.shape[1])).T * freq_ref[...]
    ct, st = jnp.cos(ang), jnp.sin(ang)
    ct_r, st_r = pltpu.roll(ct, 64, 1), pltpu.roll(st, 64, 1)
    lane = lax.broadcasted_iota(jnp.int32, ang.shape, 1)
    lo = lane < 64
    cosd_ref[...] = jnp.where(lo, ct, ct_r)
    sind_ref[...] = jnp.where(lo, -st, st_r)
    cosm_ref[...] = jnp.where(lo, ct_r, ct)
    sm = jnp.where(lo, st_r, st)
    first_half = (lane & 63) < 32
    sinlo_ref[...] = jnp.where(first_half, -sm, 0.0)
    sinhi_ref[...] = jnp.where(first_half, 0.0, sm)


def _prep_kernel(pos_ref, freq_ref, wt_ref, mem_ref, gmem_ref, wxk_ref, wxv_ref, *refs):
    *table_refs, wt_out_ref, xk_ref, xv_ref, wxk_b_ref, wxv_b_ref = refs
    _rope_table_kernel(pos_ref, freq_ref, *table_refs)
    wt_out_ref[...] = wt_ref[...].astype(BF16)

    @pl.when(pl.program_id(0) == 0)
    def _():
        wxk_b_ref[...] = wxk_ref[...].astype(BF16)
        wxv_b_ref[...] = wxv_ref[...].astype(BF16)

    hm = _rms(mem_ref[...], gmem_ref[...], NORM_EPS).astype(BF16)
    xk_ref[...] = jnp.dot(hm, wxk_b_ref[...], preferred_element_type=F32).astype(BF16)
    xv_ref[...] = jnp.dot(hm, wxv_b_ref[...], preferred_element_type=F32).astype(BF16)


def _prep(positions, w_in_t, mem2d, g_mem, w_xk, w_xv, m_len):
    batch, tile = positions.shape
    n = positions.size
    steps = batch
    rows, d = w_in_t.shape
    xw = w_xk.shape[1]
    n_slabs = max(s for s in range(1, steps + 1) if steps % s == 0 and rows % (16 * s) == 0)
    slab, revisit = rows // n_slabs, steps // n_slabs
    inv_d = ROPE_THETA ** (-jnp.arange(0, DIFF_DIM, 2, dtype=F32) / DIFF_DIM)
    inv_m = ROPE_THETA ** (-jnp.arange(0, ROPE_DIM, 2, dtype=F32) / ROPE_DIM)
    freq = jnp.concatenate([inv_d, inv_m, inv_m]).reshape(1, LANES)
    tab = jax.ShapeDtypeStruct((n, LANES), F32)
    tspec = pl.BlockSpec((tile, LANES), lambda i: (i, 0))
    wspec = pl.BlockSpec((slab, d), lambda i: (i // revisit, 0))
    const = lambda i: (0, 0)
    mem_out = pl.BlockSpec((m_len, xw), lambda i: (i, 0))
    xkv = jax.ShapeDtypeStruct((batch * m_len, xw), BF16)
    *tables, w_in_tb, xk, xv = pl.pallas_call(
        _prep_kernel,
        grid=(steps,),
        in_specs=[pl.BlockSpec((1, 1, tile), lambda i: (i, 0, 0)), pl.BlockSpec((1, LANES), const), wspec,
                  pl.BlockSpec((m_len, d), lambda i: (i, 0)), pl.BlockSpec((1, d), const),
                  pl.BlockSpec(w_xk.shape, const, pipeline_mode=pl.Buffered(1)),
                  pl.BlockSpec(w_xv.shape, const, pipeline_mode=pl.Buffered(1))],
        out_specs=[tspec] * 5 + [wspec, mem_out, mem_out],
        out_shape=[tab] * 5 + [jax.ShapeDtypeStruct((rows, d), BF16), xkv, xkv],
        scratch_shapes=[pltpu.VMEM(w_xk.shape, BF16), pltpu.VMEM(w_xv.shape, BF16)],
        compiler_params=pltpu.CompilerParams(
            dimension_semantics=("arbitrary",), vmem_limit_bytes=VMEM_LIMIT),
        name="prep",
    )(positions.reshape(batch, 1, tile), freq, w_in_t, mem2d, g_mem, w_xk, w_xv)
    return tables, w_in_tb, xk, xv


def _in_proj_kernel(x_ref, gmix_ref, w_ref, gq_ref, wuq_ref, gkv_ref, wukv_ref,
                    cosd_ref, sind_ref, cosm_ref, sinlo_ref, sinhi_ref,
                    q_ref, k_ref, v_ref, dq_ref, dk_ref, dv_ref, h_ref, wuq_b_ref, wukv_b_ref,
                    *, mla_scale, diff_scale):
    @pl.when(pl.program_id(0) == 0)
    def _():
        head_w, nope_w = NOPE_DIM + ROPE_DIM, MLA_HEADS * NOPE_DIM
        for h in range(MLA_HEADS):
            wuq_b_ref[:, h * NOPE_DIM:(h + 1) * NOPE_DIM] = (
                wuq_ref[:, h * head_w:h * head_w + NOPE_DIM].astype(BF16))
            wuq_b_ref[:, nope_w + h * ROPE_DIM:nope_w + (h + 1) * ROPE_DIM] = (
                wuq_ref[:, h * head_w + NOPE_DIM:(h + 1) * head_w].astype(BF16))
        wukv_b_ref[...] = wukv_ref[...].astype(BF16)

    def project(first_col, width):
        return _nt_dot(h_ref[...], w_ref[first_col:first_col + width, :])

    def rope64(t):
        return (t * cosm_ref[...] + pltpu.roll(t, 96, 1) * sinlo_ref[...]
                + pltpu.roll(t, 32, 1) * sinhi_ref[...])

    def rope128(t):
        return t * cosd_ref[...] + pltpu.roll(t, 64, 1) * sind_ref[...]

    h_ref[...] = _rms(x_ref[...], gmix_ref[...], NORM_EPS).astype(BF16)

    lat = project(0, LAT_DOT_W)
    c_q = _rms(lat[:, :Q_LORA], gq_ref[...], NORM_EPS).astype(BF16)
    q = jnp.dot(c_q, wuq_b_ref[...], preferred_element_type=F32) * mla_scale
    nope_w = MLA_HEADS * NOPE_DIM
    q_rope = [rope64(q[:, nope_w + c * LANES: nope_w + (c + 1) * LANES]).astype(BF16)
              for c in range(MLA_HEADS // 2)]
    c_kv = _rms(lat[:, Q_LORA:Q_LORA + KV_LORA], gkv_ref[...], NORM_EPS).astype(BF16)
    kv = jnp.dot(c_kv, wukv_b_ref[...], preferred_element_type=F32)
    kr_slab = lat[:, Q_LORA + KV_LORA:Q_LORA + KV_LORA + LANES]
    kr_lane = lax.broadcasted_iota(jnp.int32, kr_slab.shape, 1)
    kr = rope64(jnp.where(kr_lane < ROPE_DIM, kr_slab, 0.0))
    kr_even, kr_odd = kr.astype(BF16), pltpu.roll(kr, 64, 1).astype(BF16)
    ones_col = jnp.where(kr_lane == 0, 1.0, 0.0).astype(BF16)
    for h in range(MLA_HEADS):
        base = h * MLA_QK
        q_ref[:, base:base + NOPE_DIM] = q[:, h * NOPE_DIM:(h + 1) * NOPE_DIM].astype(BF16)
        q_ref[:, base + NOPE_DIM:base + MLA_QK] = q_rope[h // 2]
        k_ref[:, base:base + NOPE_DIM] = kv[:, base:base + NOPE_DIM].astype(BF16)
        k_ref[:, base + NOPE_DIM:base + MLA_QK] = kr_even if h % 2 == 0 else kr_odd
        v_ref[:, h * V_EXT:h * V_EXT + V_DIM] = kv[:, base + NOPE_DIM:base + MLA_QK].astype(BF16)
        v_ref[:, h * V_EXT + V_DIM:(h + 1) * V_EXT] = ones_col

    dq = project(LAT_W, DIFF_W)
    for c in range(dq.shape[1] // LANES):
        sl = slice(c * LANES, (c + 1) * LANES)
        dq_ref[:, sl] = (rope128(dq[:, sl]) * diff_scale).astype(BF16)
    dk = project(LAT_W + DIFF_W, DIFF_W)
    for c in range(dk.shape[1] // LANES):
        sl = slice(c * LANES, (c + 1) * LANES)
        dk_ref[:, sl] = rope128(dk[:, sl]).astype(BF16)
    dv_ref[...] = project(LAT_W + 2 * DIFF_W, DIFF_W).astype(BF16)


def _in_proj(x2d, g_mix, w_in_b, g_q, w_uq, g_kv, w_ukv, tables, tm):
    n, d = x2d.shape
    row = lambda i: (i, 0)
    const = lambda i: (0, 0)
    resident = lambda shape: pl.BlockSpec(shape, const, pipeline_mode=pl.Buffered(1))
    kern = functools.partial(
        _in_proj_kernel,
        mla_scale=(NOPE_DIM + ROPE_DIM) ** -0.5 * LOG2E,
        diff_scale=DIFF_DIM ** -0.5 * LOG2E)
    outs = [
        jax.ShapeDtypeStruct((n, MLA_HEADS * MLA_QK), BF16),
        jax.ShapeDtypeStruct((n, MLA_HEADS * MLA_QK), BF16),
        jax.ShapeDtypeStruct((n, MLA_HEADS * V_EXT), BF16),
        jax.ShapeDtypeStruct((n, DIFF_W), BF16),
        jax.ShapeDtypeStruct((n, DIFF_W), BF16),
        jax.ShapeDtypeStruct((n, DIFF_W), BF16),
    ]
    return pl.pallas_call(
        kern,
        grid=(n // tm,),
        in_specs=[
            pl.BlockSpec((tm, d), row),
            pl.BlockSpec((1, d), const),
            resident(w_in_b.shape),
            pl.BlockSpec((1, Q_LORA), const),
            resident(w_uq.shape),
            pl.BlockSpec((1, KV_LORA), const),
            resident(w_ukv.shape),
        ] + [pl.BlockSpec((tm, LANES), row)] * 5,
        out_specs=[pl.BlockSpec((tm, o.shape[1]), row) for o in outs],
        out_shape=outs,
        scratch_shapes=[pltpu.VMEM((tm, d), BF16), pltpu.VMEM(w_uq.shape, BF16), pltpu.VMEM(w_ukv.shape, BF16)],
        compiler_params=pltpu.CompilerParams(
            dimension_semantics=("arbitrary",), vmem_limit_bytes=VMEM_LIMIT),
        name="in_proj",
    )(x2d, g_mix, w_in_b, g_q, w_uq, g_kv, w_ukv, *tables)


def _cast_slab_specs(weights, n_steps, step_of):
    in_specs, out_specs, out_shapes = [], [], []
    for w in weights:
        slab = w.shape[0] // n_steps
        assert slab * n_steps == w.shape[0] and slab % 16 == 0, (w.shape, n_steps)
        spec = pl.BlockSpec((slab, w.shape[1]), lambda *ids: (step_of(*ids), 0))
        in_specs.append(spec)
        out_specs.append(spec)
        out_shapes.append(jax.ShapeDtypeStruct(w.shape, BF16))
    return in_specs, out_specs, out_shapes


def _cast_slabs(src_refs, dst_refs):
    for src, dst in zip(src_refs, dst_refs):
        dst[...] = src[...].astype(BF16)


def _mla_attn_kernel(q_ref, k_ref, v_ref, *refs, tq, n_cast):
    cast_src, (o_ref, *cast_dst) = refs[:n_cast], refs[n_cast:2 * n_cast + 1]
    s0_ref, s1_ref, p0_ref, p1_ref = refs[2 * n_cast + 1:]
    _cast_slabs(cast_src, cast_dst)
    nq = q_ref.shape[0] // tq
    s_bufs, p_bufs = (s0_ref, s1_ref), (p0_ref, p1_ref)
    rows = lambda t: slice(t * tq, (t + 1) * tq)

    def scores(t):
        s_bufs[t % 2][...] = _nt_dot(q_ref[rows(t), :], k_ref[...])

    def softmax(t):
        s = s_bufs[t % 2][...]
        p_bufs[t % 2][...] = jnp.exp2(s - jnp.max(s, axis=-1, keepdims=True)).astype(BF16)

    def pv(t):
        o = jnp.dot(p_bufs[t % 2][...], v_ref[...], preferred_element_type=F32)
        o_ref[rows(t), :] = (o[:, :V_DIM] * (1.0 / o[:, V_DIM:V_DIM + 1])).astype(o_ref.dtype)

    scores(0)
    for t in range(nq):
        if t + 1 < nq:
            scores(t + 1)
        softmax(t)
        if t >= 1:
            pv(t - 1)
    pv(nq - 1)


def _mla_attn(q, k, v, cast_weights, batch, seq, tq):
    blk = lambda w: pl.BlockSpec((seq, w), lambda b, h: (b, h))
    cast_in, cast_out, cast_shapes = _cast_slab_specs(
        cast_weights, batch * MLA_HEADS, lambda b, h: b * MLA_HEADS + h)
    return pl.pallas_call(
        functools.partial(_mla_attn_kernel, tq=tq, n_cast=len(cast_weights)),
        grid=(batch, MLA_HEADS),
        in_specs=[blk(MLA_QK), blk(MLA_QK), blk(V_EXT)] + cast_in,
        out_specs=[blk(V_DIM)] + cast_out,
        out_shape=[jax.ShapeDtypeStruct((batch * seq, MLA_HEADS * V_DIM), BF16)] + cast_shapes,
        scratch_shapes=[pltpu.VMEM((tq, seq), F32)] * 2 + [pltpu.VMEM((tq, seq), BF16)] * 2,
        compiler_params=pltpu.CompilerParams(
            dimension_semantics=("parallel", "parallel"), vmem_limit_bytes=VMEM_LIMIT),
        name="mla_attn",
    )(q, k, v, *cast_weights)


def _diff_attn_kernel(lq1_ref, lk1_ref, lq2_ref, lk2_ref, q_ref, k_ref, v_ref, g_ref, *refs,
                      tq, lambda_init, n_cast):
    cast_src, (o_ref, *cast_dst) = refs[:n_cast], refs[n_cast:2 * n_cast + 1]
    sa0_ref, sa1_ref, sb0_ref, sb1_ref, p0_ref, p1_ref = refs[2 * n_cast + 1:]
    _cast_slabs(cast_src, cast_dst)
    lam = (jnp.exp(jnp.sum(lq1_ref[...] * lk1_ref[...], axis=-1, keepdims=True))
           - jnp.exp(jnp.sum(lq2_ref[...] * lk2_ref[...], axis=-1, keepdims=True))
           + lambda_init)
    nq = q_ref.shape[0] // tq
    sa_bufs, sb_bufs, p_bufs, inv_l1 = (sa0_ref, sa1_ref), (sb0_ref, sb1_ref), (p0_ref, p1_ref), {}
    rows = lambda t: slice(t * tq, (t + 1) * tq)

    def scores(t):
        sa_bufs[t % 2][...] = _nt_dot(q_ref[rows(t), :DIFF_DIM], k_ref[:, :DIFF_DIM])
        sb_bufs[t % 2][...] = _nt_dot(q_ref[rows(t), DIFF_DIM:], k_ref[:, DIFF_DIM:])

    def softmax(t):
        e1, l1 = _softmax_parts(sa_bufs[t % 2][...])
        e2, l2 = _softmax_parts(sb_bufs[t % 2][...])
        p_bufs[t % 2][...] = (e1 - e2 * (lam * l1 / l2)).astype(BF16)
        inv_l1[t] = 1.0 / l1

    def pv(t):
        o = jnp.dot(p_bufs[t % 2][...], v_ref[...], preferred_element_type=F32) * inv_l1.pop(t)
        o = _rms(o, g_ref[...], DIFF_NORM_EPS) * (1.0 - lambda_init)
        o_ref[rows(t), :] = o.astype(o_ref.dtype)

    scores(0)
    for t in range(nq):
        if t + 1 < nq:
            scores(t + 1)
        softmax(t)
        if t >= 1:
            pv(t - 1)
    pv(nq - 1)


def _diff_attn(lams, dq, dk, dv, g_sub, cast_weights, batch, seq, tq, lambda_init):
    w = 2 * DIFF_DIM
    blk = pl.BlockSpec((seq, w), lambda b, h: (b, h))
    vec = lambda width: pl.BlockSpec((1, width), lambda b, h: (0, 0))
    cast_in, cast_out, cast_shapes = _cast_slab_specs(
        cast_weights, batch * DIFF_HEADS, lambda b, h: b * DIFF_HEADS + h)
    return pl.pallas_call(
        functools.partial(_diff_attn_kernel, tq=tq, lambda_init=lambda_init, n_cast=len(cast_weights)),
        grid=(batch, DIFF_HEADS),
        in_specs=[vec(DIFF_DIM)] * 4 + [blk, blk, blk, vec(w)] + cast_in,
        out_specs=[blk] + cast_out,
        out_shape=[jax.ShapeDtypeStruct(dv.shape, BF16)] + cast_shapes,
        scratch_shapes=[pltpu.VMEM((tq, seq), F32)] * 4 + [pltpu.VMEM((tq, seq), BF16)] * 2,
        compiler_params=pltpu.CompilerParams(
            dimension_semantics=("parallel", "parallel"), vmem_limit_bytes=VMEM_LIMIT),
        name="diff_attn",
    )(*lams, dq, dk, dv, g_sub, *cast_weights)


def _out_xattn_kernel(x_ref, a_ref, b_ref, wo_ref, gx_ref, wxq_ref, xk_ref, xv_ref, wxo_ref, o_ref,
                      *, x_scale):
    half = a_ref.shape[1]
    x1 = (x_ref[...]
          + jnp.dot(a_ref[...], wo_ref[:half, :], preferred_element_type=F32)
          + jnp.dot(b_ref[...], wo_ref[half:, :], preferred_element_type=F32))
    hx = _rms(x1, gx_ref[...], NORM_EPS).astype(BF16)
    xq = (jnp.dot(hx, wxq_ref[...], preferred_element_type=F32) * x_scale).astype(BF16)
    heads = []
    for h in range(X_HEADS):
        sl = slice(h * X_DIM, (h + 1) * X_DIM)
        e, l = _softmax_parts(_nt_dot(xq[:, sl], xk_ref[:, sl]))
        o = jnp.dot(e.astype(BF16), xv_ref[:, sl], preferred_element_type=F32)
        heads.append((o * (1.0 / l)).astype(BF16))
    xo = jnp.concatenate(heads, axis=-1)
    o_ref[...] = x1 + jnp.dot(xo, wxo_ref[...], preferred_element_type=F32)


def _out_xattn(x2d, a, b, w_out_b, g_x, w_xq_b, xk, xv, w_xo_b, seq, m_len, tm):
    n, d = x2d.shape
    xw = w_xq_b.shape[1]
    per_b = seq // tm
    row = lambda i: (i, 0)
    const = lambda i: (0, 0)
    mem = lambda i: (i // per_b, 0)
    return pl.pallas_call(
        functools.partial(_out_xattn_kernel, x_scale=X_DIM ** -0.5 * LOG2E),
        grid=(n // tm,),
        in_specs=[
            pl.BlockSpec((tm, d), row),
            pl.BlockSpec((tm, a.shape[1]), row),
            pl.BlockSpec((tm, b.shape[1]), row),
            pl.BlockSpec(w_out_b.shape, const),
            pl.BlockSpec((1, d), const),
            pl.BlockSpec(w_xq_b.shape, const),
            pl.BlockSpec((m_len, xw), mem),
            pl.BlockSpec((m_len, xw), mem),
            pl.BlockSpec(w_xo_b.shape, const),
        ],
        out_specs=pl.BlockSpec((tm, d), row),
        out_shape=jax.ShapeDtypeStruct((n, d), F32),
        compiler_params=pltpu.CompilerParams(
            dimension_semantics=("parallel",), vmem_limit_bytes=VMEM_LIMIT),
        name="out_xattn",
    )(x2d, a, b, w_out_b, g_x, w_xq_b, xk, xv, w_xo_b)


def _ffn_kernel(x_ref, g_ref, wg_ref, wu_ref, wd_ref, gf_ref, o_ref, h_ref, *, final_norm, row_split):
    j = pl.program_id(1)
    nj = pl.num_programs(1)
    tr = x_ref.shape[0] // row_split

    def step(first, last):
        for r in range(row_split):
            rows = slice(r * tr, (r + 1) * tr)
            if first:
                h_ref[rows, :] = _rms(x_ref[rows, :], g_ref[...], NORM_EPS).astype(BF16)
            h = h_ref[rows, :]
            gate = jnp.dot(h, wg_ref[...], preferred_element_type=F32)
            up = jnp.dot(h, wu_ref[...], preferred_element_type=F32)
            act = (gate * jax.nn.sigmoid(gate) * up).astype(BF16)
            y = (x_ref[rows, :] if first else o_ref[rows, :]) + jnp.dot(
                act, wd_ref[...], preferred_element_type=F32)
            if last and final_norm:
                y = _rms(y, gf_ref[...], NORM_EPS)
            o_ref[rows, :] = y

    pl.when(j == 0)(lambda: step(True, False))
    pl.when((j > 0) & (j < nj - 1))(lambda: step(False, False))
    pl.when(j == nj - 1)(lambda: step(False, True))


def _ffn(x2d, g_ffn, w_gate_b, w_up_b, w_down_b, g_final, tm, th, final_norm):
    n, d = x2d.shape
    hidden = w_gate_b.shape[1]
    assert hidden // th >= 2
    row = lambda i, j: (i, 0)
    const = lambda i, j: (0, 0)
    return pl.pallas_call(
        functools.partial(_ffn_kernel, final_norm=final_norm, row_split=2),
        grid=(n // tm, hidden // th),
        in_specs=[
            pl.BlockSpec((tm, d), row),
            pl.BlockSpec((1, d), const),
            pl.BlockSpec((d, th), lambda i, j: (0, j)),
            pl.BlockSpec((d, th), lambda i, j: (0, j)),
            pl.BlockSpec((th, d), lambda i, j: (j, 0)),
            pl.BlockSpec((1, d), const),
        ],
        out_specs=pl.BlockSpec((tm, d), row),
        out_shape=jax.ShapeDtypeStruct((n, d), F32),
        scratch_shapes=[pltpu.VMEM((tm, d), BF16)],
        compiler_params=pltpu.CompilerParams(
            dimension_semantics=("parallel", "arbitrary"), vmem_limit_bytes=VMEM_LIMIT),
        name="ffn",
    )(x2d, g_ffn, w_gate_b, w_up_b, w_down_b, g_final)


def kernel(x, mem, positions, g_mix, w_in, g_q_lat, w_uq, g_kv_lat, w_ukv, lambda_q1, lambda_k1,
           lambda_q2, lambda_k2, g_diff_sub, w_out, g_xattn, g_mem, w_xq, w_xk, w_xv, w_xo,
           g_ffn, w_gate, w_up, w_down, g_final):
    batch, seq, d = x.shape
    m_len = mem.shape[1]
    depth = w_in.shape[0]
    n = batch * seq

    x2d = x.reshape(n, d)
    mem2d = mem.reshape(batch * m_len, d)
    row = lambda v: v.reshape(1, -1)

    for layer in range(depth):
        lambda_init = 0.8 - 0.6 * math.exp(-0.3 * layer)
        tables, w_in_b, xk, xv = _prep(
            positions, jnp.swapaxes(w_in[layer], 0, 1), mem2d, row(g_mem[layer]),
            w_xk[layer], w_xv[layer], m_len)
        q, k, v, dq, dk, dv = _in_proj(
            x2d, row(g_mix[layer]), w_in_b, row(g_q_lat[layer]),
            w_uq[layer], row(g_kv_lat[layer]), w_ukv[layer],
            tables, tm=_Tiles.in_proj_tokens)
        out_mla, w_gate_b, w_up_b = _mla_attn(
            q, k, v, [w_gate[layer], w_up[layer]], batch, seq, tq=_Tiles.mla_queries)
        out_diff, w_down_b, w_out_b, w_xq_b, w_xo_b = _diff_attn(
            [row(lambda_q1[layer]), row(lambda_k1[layer]), row(lambda_q2[layer]), row(lambda_k2[layer])],
            dq, dk, dv, row(g_diff_sub[layer]),
            [w_down[layer], w_out[layer], w_xq[layer], w_xo[layer]],
            batch, seq, tq=_Tiles.diff_queries, lambda_init=lambda_init)
        x2d = _out_xattn(x2d, out_mla, out_diff, w_out_b, row(g_xattn[layer]),
                         w_xq_b, xk, xv, w_xo_b, seq, m_len, tm=_Tiles.out_xattn_tokens)
        x2d = _ffn(x2d, row(g_ffn[layer]), w_gate_b, w_up_b, w_down_b, row(g_final),
                   tm=_Tiles.ffn_tokens, th=_Tiles.ffn_hidden, final_norm=(layer == depth - 1))
    return x2d.reshape(batch, seq, d)
```

```python
import functools
import math

import jax
import jax.numpy as jnp
from jax import lax
from jax.experimental import pallas as pl
from jax.experimental.pallas import tpu as pltpu

F32 = jnp.float32
BF16 = jnp.bfloat16

ROPE_THETA = 10000.0
NORM_EPS = 1e-6
DIFF_NORM_EPS = 1e-5
Q_LORA, KV_LORA, ROPE_DIM = 512, 256, 64
MLA_HEADS, NOPE_DIM, V_DIM = 8, 128, 128
DIFF_HEADS, DIFF_DIM = 4, 128
X_HEADS, X_DIM = 4, 128
LANES = 128
LOG2E = math.log2(math.e)
VMEM_LIMIT = 58 * 1024 * 1024


class _Tiles:
    in_proj_tokens = 512
    mla_queries = 1024
    diff_queries = 512
    out_xattn_tokens = 512
    ffn_tokens = 1024
    ffn_hidden = 512

LAT_W = Q_LORA + KV_LORA + ROPE_DIM
LAT_DOT_W = Q_LORA + KV_LORA + LANES
DIFF_W = 2 * DIFF_HEADS * DIFF_DIM
MLA_QK = 2 * LANES
V_EXT = 2 * LANES


def _rms(x, g, eps):
    return x * lax.rsqrt(jnp.mean(x * x, axis=-1, keepdims=True) + eps) * g


def _nt_dot(a, b):
    return lax.dot_general(a, b, (((1,), (1,)), ((), ())), preferred_element_type=F32)


def _softmax_parts(s):
    m = jnp.max(s, axis=-1, keepdims=True)
    e = jnp.exp2(s - m)
    return e, jnp.sum(e, axis=-1, keepdims=True)


def _rope_table_kernel(pos_ref, freq_ref, cosd_ref, sind_ref, cosm_ref, sinlo_ref, sinhi_ref):
    pos_row = pos_ref[0].astype(F32)
    ang = jnp.broadcast_to(pos_row, (LANES, pos_row.shape[1])).T * freq_ref[...]
    ct, st = jnp.cos(ang), jnp.sin(ang)
    ct_r, st_r = pltpu.roll(ct, 64, 1), pltpu.roll(st, 64, 1)
    lane = lax.broadcasted_iota(jnp.int32, ang.shape, 1)
    lo = lane < 64
    cosd_ref[...] = jnp.where(lo, ct, ct_r)
    sind_ref[...] = jnp.where(lo, -st, st_r)
    cosm_ref[...] = jnp.where(lo, ct_r, ct)
    sm = jnp.where(lo, st_r, st)
    first_half = (lane & 63) < 32
    sinlo_ref[...] = jnp.where(first_half, -sm, 0.0)
    sinhi_ref[...] = jnp.where(first_half, 0.0, sm)


def _prep_kernel(pos_ref, freq_ref, wt_ref, mem_ref, gmem_ref, wxk_ref, wxv_ref, *refs):
    *table_refs, wt_out_ref, xk_ref, xv_ref, wxk_b_ref, wxv_b_ref = refs
    _rope_table_kernel(pos_ref, freq_ref, *table_refs)
    wt_out_ref[...] = wt_ref[...].astype(BF16)

    @pl.when(pl.program_id(0) == 0)
    def _():
        wxk_b_ref[...] = wxk_ref[...].astype(BF16)
        wxv_b_ref[...] = wxv_ref[...].astype(BF16)

    hm = _rms(mem_ref[...], gmem_ref[...], NORM_EPS).astype(BF16)
    xk_ref[...] = jnp.dot(hm, wxk_b_ref[...], preferred_element_type=F32).astype(BF16)
    xv_ref[...] = jnp.dot(hm, wxv_b_ref[...], preferred_element_type=F32).astype(BF16)


def _prep(positions, w_in_t, mem2d, g_mem, w_xk, w_xv, m_len):
    batch, tile = positions.shape
    n = positions.size
    steps = batch
    rows, d = w_in_t.shape
    xw = w_xk.shape[1]
    n_slabs = max(s for s in range(1, steps + 1) if steps % s == 0 and rows % (16 * s) == 0)
    slab, revisit = rows // n_slabs, steps // n_slabs
    inv_d = ROPE_THETA ** (-jnp.arange(0, DIFF_DIM, 2, dtype=F32) / DIFF_DIM)
    inv_m = ROPE_THETA ** (-jnp.arange(0, ROPE_DIM, 2, dtype=F32) / ROPE_DIM)
    freq = jnp.concatenate([inv_d, inv_m, inv_m]).reshape(1, LANES)
    tab = jax.ShapeDtypeStruct((n, LANES), F32)
    tspec = pl.BlockSpec((tile, LANES), lambda i: (i, 0))
    wspec = pl.BlockSpec((slab, d), lambda i: (i // revisit, 0))
    const = lambda i: (0, 0)
    mem_out = pl.BlockSpec((m_len, xw), lambda i: (i, 0))
    xkv = jax.ShapeDtypeStruct((batch * m_len, xw), BF16)
    *tables, w_in_tb, xk, xv = pl.pallas_call(
        _prep_kernel,
        grid=(steps,),
        in_specs=[pl.BlockSpec((1, 1, tile), lambda i: (i, 0, 0)), pl.BlockSpec((1, LANES), const), wspec,
                  pl.BlockSpec((m_len, d), lambda i: (i, 0)), pl.BlockSpec((1, d), const),
                  pl.BlockSpec(w_xk.shape, const, pipeline_mode=pl.Buffered(1)),
                  pl.BlockSpec(w_xv.shape, const, pipeline_mode=pl.Buffered(1))],
        out_specs=[tspec] * 5 + [wspec, mem_out, mem_out],
        out_shape=[tab] * 5 + [jax.ShapeDtypeStruct((rows, d), BF16), xkv, xkv],
        scratch_shapes=[pltpu.VMEM(w_xk.shape, BF16), pltpu.VMEM(w_xv.shape, BF16)],
        compiler_params=pltpu.CompilerParams(
            dimension_semantics=("arbitrary",), vmem_limit_bytes=VMEM_LIMIT),
        name="prep",
    )(positions.reshape(batch, 1, tile), freq, w_in_t, mem2d, g_mem, w_xk, w_xv)
    return tables, w_in_tb, xk, xv


def _in_proj_kernel(x_ref, gmix_ref, w_ref, gq_ref, wuq_ref, gkv_ref, wukv_ref,
                    cosd_ref, sind_ref, cosm_ref, sinlo_ref, sinhi_ref,
                    q_ref, k_ref, v_ref, dq_ref, dk_ref, dv_ref, h_ref, wuq_b_ref, wukv_b_ref,
                    *, mla_scale, diff_scale):
    @pl.when(pl.program_id(0) == 0)
    def _():
        head_w, nope_w = NOPE_DIM + ROPE_DIM, MLA_HEADS * NOPE_DIM
        for h in range(MLA_HEADS):
            wuq_b_ref[:, h * NOPE_DIM:(h + 1) * NOPE_DIM] = (
                wuq_ref[:, h * head_w:h * head_w + NOPE_DIM].astype(BF16))
            wuq_b_ref[:, nope_w + h * ROPE_DIM:nope_w + (h + 1) * ROPE_DIM] = (
                wuq_ref[:, h * head_w + NOPE_DIM:(h + 1) * head_w].astype(BF16))
        wukv_b_ref[...] = wukv_ref[...].astype(BF16)

    def project(first_col, width):
        return _nt_dot(h_ref[...], w_ref[first_col:first_col + width, :])

    def rope64(t):
        return (t * cosm_ref[...] + pltpu.roll(t, 96, 1) * sinlo_ref[...]
                + pltpu.roll(t, 32, 1) * sinhi_ref[...])

    def rope128(t):
        return t * cosd_ref[...] + pltpu.roll(t, 64, 1) * sind_ref[...]

    h_ref[...] = _rms(x_ref[...], gmix_ref[...], NORM_EPS).astype(BF16)

    lat = project(0, LAT_DOT_W)
    c_q = _rms(lat[:, :Q_LORA], gq_ref[...], NORM_EPS).astype(BF16)
    q = jnp.dot(c_q, wuq_b_ref[...], preferred_element_type=F32) * mla_scale
    nope_w = MLA_HEADS * NOPE_DIM
    q_rope = [rope64(q[:, nope_w + c * LANES: nope_w + (c + 1) * LANES]).astype(BF16)
              for c in range(MLA_HEADS // 2)]
    c_kv = _rms(lat[:, Q_LORA:Q_LORA + KV_LORA], gkv_ref[...], NORM_EPS).astype(BF16)
    kv = jnp.dot(c_kv, wukv_b_ref[...], preferred_element_type=F32)
    kr_slab = lat[:, Q_LORA + KV_LORA:Q_LORA + KV_LORA + LANES]
    kr_lane = lax.broadcasted_iota(jnp.int32, kr_slab.shape, 1)
    kr = rope64(jnp.where(kr_lane < ROPE_DIM, kr_slab, 0.0))
    kr_even, kr_odd = kr.astype(BF16), pltpu.roll(kr, 64, 1).astype(BF16)
    ones_col = jnp.where(kr_lane == 0, 1.0, 0.0).astype(BF16)
    for h in range(MLA_HEADS):
        base = h * MLA_QK
        q_ref[:, base:base + NOPE_DIM] = q[:, h * NOPE_DIM:(h + 1) * NOPE_DIM].astype(BF16)
        q_ref[:, base + NOPE_DIM:base + MLA_QK] = q_rope[h // 2]
        k_ref[:, base:base + NOPE_DIM] = kv[:, base:base + NOPE_DIM].astype(BF16)
        k_ref[:, base + NOPE_DIM:base + MLA_QK] = kr_even if h % 2 == 0 else kr_odd
        v_ref[:, h * V_EXT:h * V_EXT + V_DIM] = kv[:, base + NOPE_DIM:base + MLA_QK].astype(BF16)
        v_ref[:, h * V_EXT + V_DIM:(h + 1) * V_EXT] = ones_col

    dq = project(LAT_W, DIFF_W)
    for c in range(dq.shape[1] // LANES):
        sl = slice(c * LANES, (c + 1) * LANES)
        dq_ref[:, sl] = (rope128(dq[:, sl]) * diff_scale).astype(BF16)
    dk = project(LAT_W + DIFF_W, DIFF_W)
    for c in range(dk.shape[1] // LANES):
        sl = slice(c * LANES, (c + 1) * LANES)
        dk_ref[:, sl] = rope128(dk[:, sl]).astype(BF16)
    dv_ref[...] = project(LAT_W + 2 * DIFF_W, DIFF_W).astype(BF16)


def _in_proj(x2d, g_mix, w_in_b, g_q, w_uq, g_kv, w_ukv, tables, tm):
    n, d = x2d.shape
    row = lambda i: (i, 0)
    const = lambda i: (0, 0)
    resident = lambda shape: pl.BlockSpec(shape, const, pipeline_mode=pl.Buffered(1))
    kern = functools.partial(
        _in_proj_kernel,
        mla_scale=(NOPE_DIM + ROPE_DIM) ** -0.5 * LOG2E,
        diff_scale=DIFF_DIM ** -0.5 * LOG2E)
    outs = [
        jax.ShapeDtypeStruct((n, MLA_HEADS * MLA_QK), BF16),
        jax.ShapeDtypeStruct((n, MLA_HEADS * MLA_QK), BF16),
        jax.ShapeDtypeStruct((n, MLA_HEADS * V_EXT), BF16),
        jax.ShapeDtypeStruct((n, DIFF_W), BF16),
        jax.ShapeDtypeStruct((n, DIFF_W), BF16),
        jax.ShapeDtypeStruct((n, DIFF_W), BF16),
    ]
    return pl.pallas_call(
        kern,
        grid=(n // tm,),
        in_specs=[
            pl.BlockSpec((tm, d), row),
            pl.BlockSpec((1, d), const),
            resident(w_in_b.shape),
            pl.BlockSpec((1, Q_LORA), const),
            resident(w_uq.shape),
            pl.BlockSpec((1, KV_LORA), const),
            resident(w_ukv.shape),
        ] + [pl.BlockSpec((tm, LANES), row)] * 5,
        out_specs=[pl.BlockSpec((tm, o.shape[1]), row) for o in outs],
        out_shape=outs,
        scratch_shapes=[pltpu.VMEM((tm, d), BF16), pltpu.VMEM(w_uq.shape, BF16), pltpu.VMEM(w_ukv.shape, BF16)],
        compiler_params=pltpu.CompilerParams(
            dimension_semantics=("arbitrary",), vmem_limit_bytes=VMEM_LIMIT),
        name="in_proj",
    )(x2d, g_mix, w_in_b, g_q, w_uq, g_kv, w_ukv, *tables)


def _cast_slab_specs(weights, n_steps, step_of):
    in_specs, out_specs, out_shapes = [], [], []
    for w in weights:
        slab = w.shape[0] // n_steps
        assert slab * n_steps == w.shape[0] and slab % 16 == 0, (w.shape, n_steps)
        spec = pl.BlockSpec((slab, w.shape[1]), lambda *ids: (step_of(*ids), 0))
        in_specs.append(spec)
        out_specs.append(spec)
        out_shapes.append(jax.ShapeDtypeStruct(w.shape, BF16))
    return in_specs, out_specs, out_shapes


def _cast_slabs(src_refs, dst_refs):
    for src, dst in zip(src_refs, dst_refs):
        dst[...] = src[...].astype(BF16)


def _mla_attn_kernel(q_ref, k_ref, v_ref, *refs, tq, n_cast):
    cast_src, (o_ref, *cast_dst) = refs[:n_cast], refs[n_cast:2 * n_cast + 1]
    s0_ref, s1_ref, p0_ref, p1_ref = refs[2 * n_cast + 1:]
    _cast_slabs(cast_src, cast_dst)
    nq = q_ref.shape[0] // tq
    s_bufs, p_bufs = (s0_ref, s1_ref), (p0_ref, p1_ref)
    rows = lambda t: slice(t * tq, (t + 1) * tq)

    def scores(t):
        s_bufs[t % 2][...] = _nt_dot(q_ref[rows(t), :], k_ref[...])

    def softmax(t):
        s = s_bufs[t % 2][...]
        p_bufs[t % 2][...] = jnp.exp2(s - jnp.max(s, axis=-1, keepdims=True)).astype(BF16)

    def pv(t):
        o = jnp.dot(p_bufs[t % 2][...], v_ref[...], preferred_element_type=F32)
        o_ref[rows(t), :] = (o[:, :V_DIM] * (1.0 / o[:, V_DIM:V_DIM + 1])).astype(o_ref.dtype)

    scores(0)
    for t in range(nq):
        if t + 1 < nq:
            scores(t + 1)
        softmax(t)
        if t >= 1:
            pv(t - 1)
    pv(nq - 1)


def _mla_attn(q, k, v, cast_weights, batch, seq, tq):
    blk = lambda w: pl.BlockSpec((seq, w), lambda b, h: (b, h))
    cast_in, cast_out, cast_shapes = _cast_slab_specs(
        cast_weights, batch * MLA_HEADS, lambda b, h: b * MLA_HEADS + h)
    return pl.pallas_call(
        functools.partial(_mla_attn_kernel, tq=tq, n_cast=len(cast_weights)),
        grid=(batch, MLA_HEADS),
        in_specs=[blk(MLA_QK), blk(MLA_QK), blk(V_EXT)] + cast_in,
        out_specs=[blk(V_DIM)] + cast_out,
        out_shape=[jax.ShapeDtypeStruct((batch * seq, MLA_HEADS * V_DIM), BF16)] + cast_shapes,
        scratch_shapes=[pltpu.VMEM((tq, seq), F32)] * 2 + [pltpu.VMEM((tq, seq), BF16)] * 2,
        compiler_params=pltpu.CompilerParams(
            dimension_semantics=("parallel", "parallel"), vmem_limit_bytes=VMEM_LIMIT),
        name="mla_attn",
    )(q, k, v, *cast_weights)


def _diff_attn_kernel(lq1_ref, lk1_ref, lq2_ref, lk2_ref, q_ref, k_ref, v_ref, g_ref, *refs,
                      tq, lambda_init, n_cast):
    cast_src, (o_ref, *cast_dst) = refs[:n_cast], refs[n_cast:2 * n_cast + 1]
    sa0_ref, sa1_ref, sb0_ref, sb1_ref, p0_ref, p1_ref = refs[2 * n_cast + 1:]
    _cast_slabs(cast_src, cast_dst)
    lam = (jnp.exp(jnp.sum(lq1_ref[...] * lk1_ref[...], axis=-1, keepdims=True))
           - jnp.exp(jnp.sum(lq2_ref[...] * lk2_ref[...], axis=-1, keepdims=True))
           + lambda_init)
    nq = q_ref.shape[0] // tq
    sa_bufs, sb_bufs, p_bufs, inv_l1 = (sa0_ref, sa1_ref), (sb0_ref, sb1_ref), (p0_ref, p1_ref), {}
    rows = lambda t: slice(t * tq, (t + 1) * tq)

    def scores(t):
        sa_bufs[t % 2][...] = _nt_dot(q_ref[rows(t), :DIFF_DIM], k_ref[:, :DIFF_DIM])
        sb_bufs[t % 2][...] = _nt_dot(q_ref[rows(t), DIFF_DIM:], k_ref[:, DIFF_DIM:])

    def softmax(t):
        e1, l1 = _softmax_parts(sa_bufs[t % 2][...])
        e2, l2 = _softmax_parts(sb_bufs[t % 2][...])
        p_bufs[t % 2][...] = (e1 - e2 * (lam * l1 / l2)).astype(BF16)
        inv_l1[t] = 1.0 / l1

    def pv(t):
        o = jnp.dot(p_bufs[t % 2][...], v_ref[...], preferred_element_type=F32) * inv_l1.pop(t)
        o = _rms(o, g_ref[...], DIFF_NORM_EPS) * (1.0 - lambda_init)
        o_ref[rows(t), :] = o.astype(o_ref.dtype)

    scores(0)
    for t in range(nq):
        softmax(t)
        if t >= 1:
            pv(t - 1)
        if t + 1 < nq:
            scores(t + 1)
    pv(nq - 1)


def _diff_attn(lams, dq, dk, dv, g_sub, cast_weights, batch, seq, tq, lambda_init):
    w = 2 * DIFF_DIM
    blk = pl.BlockSpec((seq, w), lambda b, h: (b, h))
    vec = lambda width: pl.BlockSpec((1, width), lambda b, h: (0, 0))
    cast_in, cast_out, cast_shapes = _cast_slab_specs(
        cast_weights, batch * DIFF_HEADS, lambda b, h: b * DIFF_HEADS + h)
    return pl.pallas_call(
        functools.partial(_diff_attn_kernel, tq=tq, lambda_init=lambda_init, n_cast=len(cast_weights)),
        grid=(batch, DIFF_HEADS),
        in_specs=[vec(DIFF_DIM)] * 4 + [blk, blk, blk, vec(w)] + cast_in,
        out_specs=[blk] + cast_out,
        out_shape=[jax.ShapeDtypeStruct(dv.shape, BF16)] + cast_shapes,
        scratch_shapes=[pltpu.VMEM((tq, seq), F32)] * 4 + [pltpu.VMEM((tq, seq), BF16)] * 2,
        compiler_params=pltpu.CompilerParams(
            dimension_semantics=("parallel", "parallel"), vmem_limit_bytes=VMEM_LIMIT),
        name="diff_attn",
    )(*lams, dq, dk, dv, g_sub, *cast_weights)


def _out_xattn_kernel(x_ref, a_ref, b_ref, wo_ref, gx_ref, wxq_ref, xk_ref, xv_ref, wxo_ref, o_ref,
                      *, x_scale):
    half = a_ref.shape[1]
    x1 = (x_ref[...]
          + jnp.dot(a_ref[...], wo_ref[:half, :], preferred_element_type=F32)
          + jnp.dot(b_ref[...], wo_ref[half:, :], preferred_element_type=F32))
    hx = _rms(x1, gx_ref[...], NORM_EPS).astype(BF16)
    xq = (jnp.dot(hx, wxq_ref[...], preferred_element_type=F32) * x_scale).astype(BF16)
    heads = []
    for h in range(X_HEADS):
        sl = slice(h * X_DIM, (h + 1) * X_DIM)
        e, l = _softmax_parts(_nt_dot(xq[:, sl], xk_ref[:, sl]))
        o = jnp.dot(e.astype(BF16), xv_ref[:, sl], preferred_element_type=F32)
        heads.append((o * (1.0 / l)).astype(BF16))
    xo = jnp.concatenate(heads, axis=-1)
    o_ref[...] = x1 + jnp.dot(xo, wxo_ref[...], preferred_element_type=F32)


def _out_xattn(x2d, a, b, w_out_b, g_x, w_xq_b, xk, xv, w_xo_b, seq, m_len, tm):
    n, d = x2d.shape
    xw = w_xq_b.shape[1]
    per_b = seq // tm
    row = lambda i: (i, 0)
    const = lambda i: (0, 0)
    mem = lambda i: (i // per_b, 0)
    return pl.pallas_call(
        functools.partial(_out_xattn_kernel, x_scale=X_DIM ** -0.5 * LOG2E),
        grid=(n // tm,),
        in_specs=[
            pl.BlockSpec((tm, d), row),
            pl.BlockSpec((tm, a.shape[1]), row),
            pl.BlockSpec((tm, b.shape[1]), row),
            pl.BlockSpec(w_out_b.shape, const),
            pl.BlockSpec((1, d), const),
            pl.BlockSpec(w_xq_b.shape, const),
            pl.BlockSpec((m_len, xw), mem),
            pl.BlockSpec((m_len, xw), mem),
            pl.BlockSpec(w_xo_b.shape, const),
        ],
        out_specs=pl.BlockSpec((tm, d), row),
        out_shape=jax.ShapeDtypeStruct((n, d), F32),
        compiler_params=pltpu.CompilerParams(
            dimension_semantics=("parallel",), vmem_limit_bytes=VMEM_LIMIT),
        name="out_xattn",
    )(x2d, a, b, w_out_b, g_x, w_xq_b, xk, xv, w_xo_b)


def _ffn_kernel(x_ref, g_ref, wg_ref, wu_ref, wd_ref, gf_ref, o_ref, h_ref, *, final_norm, row_split):
    j = pl.program_id(1)
    nj = pl.num_programs(1)

    def step(first, last):
        groups = row_split if (first or last) else 1
        tr = x_ref.shape[0] // groups
        for r in range(groups):
            rows = slice(r * tr, (r + 1) * tr)
            if first:
                h_ref[rows, :] = _rms(x_ref[rows, :], g_ref[...], NORM_EPS).astype(BF16)
            h = h_ref[rows, :]
            gate = jnp.dot(h, wg_ref[...], preferred_element_type=F32)
            up = jnp.dot(h, wu_ref[...], preferred_element_type=F32)
            act = (gate * jax.nn.sigmoid(gate) * up).astype(BF16)
            y = (x_ref[rows, :] if first else o_ref[rows, :]) + jnp.dot(
                act, wd_ref[...], preferred_element_type=F32)
            if last and final_norm:
                y = _rms(y, gf_ref[...], NORM_EPS)
            o_ref[rows, :] = y

    pl.when(j == 0)(lambda: step(True, False))
    pl.when((j > 0) & (j < nj - 1))(lambda: step(False, False))
    pl.when(j == nj - 1)(lambda: step(False, True))


def _ffn(x2d, g_ffn, w_gate_b, w_up_b, w_down_b, g_final, tm, th, final_norm):
    n, d = x2d.shape
    hidden = w_gate_b.shape[1]
    assert hidden // th >= 2
    row = lambda i, j: (i, 0)
    const = lambda i, j: (0, 0)
    return pl.pallas_call(
        functools.partial(_ffn_kernel, final_norm=final_norm, row_split=2),
        grid=(n // tm, hidden // th),
        in_specs=[
            pl.BlockSpec((tm, d), row),
            pl.BlockSpec((1, d), const),
            pl.BlockSpec((d, th), lambda i, j: (0, j)),
            pl.BlockSpec((d, th), lambda i, j: (0, j)),
            pl.BlockSpec((th, d), lambda i, j: (j, 0)),
            pl.BlockSpec((1, d), const),
        ],
        out_specs=pl.BlockSpec((tm, d), row),
        out_shape=jax.ShapeDtypeStruct((n, d), F32),
        scratch_shapes=[pltpu.VMEM((tm, d), BF16)],
        compiler_params=pltpu.CompilerParams(
            dimension_semantics=("parallel", "arbitrary"), vmem_limit_bytes=VMEM_LIMIT),
        name="ffn",
    )(x2d, g_ffn, w_gate_b, w_up_b, w_down_b, g_final)


def kernel(x, mem, positions, g_mix, w_in, g_q_lat, w_uq, g_kv_lat, w_ukv, lambda_q1, lambda_k1,
           lambda_q2, lambda_k2, g_diff_sub, w_out, g_xattn, g_mem, w_xq, w_xk, w_xv, w_xo,
           g_ffn, w_gate, w_up, w_down, g_final):
    batch, seq, d = x.shape
    m_len = mem.shape[1]
    depth = w_in.shape[0]
    n = batch * seq

    x2d = x.reshape(n, d)
    mem2d = mem.reshape(batch * m_len, d)
    row = lambda v: v.reshape(1, -1)

    for layer in range(depth):
        lambda_init = 0.8 - 0.6 * math.exp(-0.3 * layer)
        tables, w_in_b, xk, xv = _prep(
            positions, jnp.swapaxes(w_in[layer], 0, 1), mem2d, row(g_mem[layer]),
            w_xk[layer], w_xv[layer], m_len)
        q, k, v, dq, dk, dv = _in_proj(
            x2d, row(g_mix[layer]), w_in_b, row(g_q_lat[layer]),
            w_uq[layer], row(g_kv_lat[layer]), w_ukv[layer],
            tables, tm=_Tiles.in_proj_tokens)
        out_mla, w_gate_b, w_up_b = _mla_attn(
            q, k, v, [w_gate[layer], w_up[layer]], batch, seq, tq=_Tiles.mla_queries)
        out_diff, w_down_b, w_out_b, w_xq_b, w_xo_b = _diff_attn(
            [row(lambda_q1[layer]), row(lambda_k1[layer]), row(lambda_q2[layer]), row(lambda_k2[layer])],
            dq, dk, dv, row(g_diff_sub[layer]),
            [w_down[layer], w_out[layer], w_xq[layer], w_xo[layer]],
            batch, seq, tq=_Tiles.diff_queries, lambda_init=lambda_init)
        x2d = _out_xattn(x2d, out_mla, out_diff, w_out_b, row(g_xattn[layer]),
                         w_xq_b, xk, xv, w_xo_b, seq, m_len, tm=_Tiles.out_xattn_tokens)
        x2d = _ffn(x2d, row(g_ffn[layer]), w_gate_b, w_up_b, w_down_b, row(g_final),
                   tm=_Tiles.ffn_tokens, th=_Tiles.ffn_hidden, final_norm=(layer == depth - 1))
    return x2d.reshape(batch, seq, d)
```
